```python
import math
import jax, jax.numpy as jnp
from jax import lax
import numpy as np

D_MODEL = 2048
BATCH = 1
SEQ = 8192
DEPTH = 4

CHUNK = 64
Q_BLOCK = 128
NORM_EPS = 1e-6
N_BRANCH = 3

RET_HEAD_DIM = 128
RET_WIDTH = D_MODEL // 2
RET_HEADS = RET_WIDTH // RET_HEAD_DIM
ROPE_BASE = 10000.0

SB_HEAD_DIM = 128
SB_WIDTH = D_MODEL // 2
SB_HEADS = SB_WIDTH // SB_HEAD_DIM

SSD_HEAD_DIM = 64
SSD_WIDTH = D_MODEL // 2
SSD_HEADS = SSD_WIDTH // SSD_HEAD_DIM
SSD_GROUPS = 4
SSD_HEADS_PER_GROUP = SSD_HEADS // SSD_GROUPS
SSD_STATE = 128
SSD_CONV = 4
SSD_CONV_DIM = SSD_WIDTH + 2 * SSD_GROUPS * SSD_STATE

FFN_HIDDEN = ((8 * D_MODEL // 3 + 255) // 256) * 256

IN_SIZES = (RET_WIDTH, RET_WIDTH, RET_WIDTH, RET_WIDTH,
            SB_WIDTH, SB_WIDTH, SB_WIDTH,
            SSD_WIDTH, SSD_CONV_DIM, SSD_HEADS,
            N_BRANCH * D_MODEL)
IN_COLS = sum(IN_SIZES)

kernel_name = 'hybrid_retention_stickbreaking_ssd_trunk'


def rms_norm(x, w):
    xf = x.astype(jnp.float32)
    y = xf * lax.rsqrt(jnp.mean(xf * xf, axis=-1, keepdims=True) + NORM_EPS)
    return (y * w.astype(jnp.float32)).astype(x.dtype)


def apply_rotary(t, positions):
    half = t.shape[-1] // 2
    inv_freq = ROPE_BASE ** (-2.0 * jnp.arange(half, dtype=jnp.float32) / t.shape[-1])
    ang = positions.astype(jnp.float32)[:, :, None] * inv_freq
    cos = jnp.cos(ang)[:, :, None, :]
    sin = jnp.sin(ang)[:, :, None, :]
    t1, t2 = t[..., :half], t[..., half:]
    return jnp.concatenate([t1 * cos - t2 * sin, t1 * sin + t2 * cos], axis=-1)


def retention_mixer(q, k, v, g, positions, gn_w):
    out_dtype = q.dtype
    b, s, _ = q.shape
    nc = s // CHUNK
    H, Dh = RET_HEADS, RET_HEAD_DIM
    q = apply_rotary(q.astype(jnp.float32).reshape(b, s, H, Dh), positions)
    k = apply_rotary(k.astype(jnp.float32).reshape(b, s, H, Dh), positions) * (Dh ** -0.5)
    v = v.astype(jnp.float32).reshape(b, s, H, Dh)
    log_gamma = jnp.log1p(-jnp.exp2(-5.0 - jnp.arange(H, dtype=jnp.float32)))
    idx = jnp.arange(CHUNK, dtype=jnp.float32)
    intra_decay = jnp.exp(log_gamma[:, None, None] * jnp.abs(idx[:, None] - idx[None, :]))
    q = q.reshape(b, nc, CHUNK, H, Dh)
    k = k.reshape(b, nc, CHUNK, H, Dh)
    v = v.reshape(b, nc, CHUNK, H, Dh)
    scores = jnp.einsum('bcihd,bcjhd->bchij', q, k) * intra_decay
    o_intra = jnp.einsum('bchij,bcjhd->bcihd', scores, v)
    q_decay = jnp.exp(log_gamma[None, :] * (idx[:, None] + 1.0))
    k_decay = jnp.exp(log_gamma[None, :] * (CHUNK - 1.0 - idx[:, None]))
    chunk_decay = jnp.exp(log_gamma * CHUNK)

    def step(state, inp):
        qc, kc, vc = inp
        cross = jnp.einsum('bihd,bhde->bihe', qc * q_decay[None, :, :, None], state)
        state = state * chunk_decay[None, :, None, None] + jnp.einsum(
            'bjhd,bjhe->bhde', kc * k_decay[None, :, :, None], vc)
        return state, cross

    state0 = jnp.zeros((b, H, Dh, Dh), jnp.float32)
    _, o_cross = lax.scan(step, state0, (jnp.moveaxis(q, 1, 0), jnp.moveaxis(k, 1, 0), jnp.moveaxis(v, 1, 0)))
    o = (o_intra + jnp.moveaxis(o_cross, 0, 1)).reshape(b, s, H, Dh)
    mu = jnp.mean(o, axis=-1, keepdims=True)
    var = jnp.mean(jnp.square(o - mu), axis=-1, keepdims=True)
    o = (o - mu) * lax.rsqrt(var + NORM_EPS) * gn_w.astype(jnp.float32).reshape(H, Dh)
    o = o.reshape(b, s, RET_WIDTH) * jax.nn.silu(g.astype(jnp.float32))
    return o.astype(out_dtype)


def stick_breaking_mixer(q, k, v):
    b, s, _ = q.shape
    H, Dh = SB_HEADS, SB_HEAD_DIM
    q = q.reshape(b, s, H, Dh).transpose(0, 2, 1, 3)
    k = k.reshape(b, s, H, Dh).transpose(0, 2, 1, 3)
    v = v.reshape(b, s, H, Dh).transpose(0, 2, 1, 3)
    key_pos = jnp.arange(s)
    scale = Dh ** -0.5

    def block(i):
        qb = lax.dynamic_slice_in_dim(q, i * Q_BLOCK, Q_BLOCK, axis=2)
        z = jnp.einsum('bhqd,bhkd->bhqk', qb, k).astype(jnp.float32) * scale
        t = i * Q_BLOCK + jnp.arange(Q_BLOCK)
        mask = key_pos[None, :] < t[:, None]
        log_keep = jnp.where(mask, jax.nn.log_sigmoid(-z), 0.0)
        later = lax.cumsum(log_keep, axis=3, reverse=True) - log_keep
        w = jnp.where(mask, jnp.exp(jax.nn.log_sigmoid(z) + later), 0.0)
        return jnp.einsum('bhqk,bhkd->bhqd', w.astype(v.dtype), v)

    out = lax.map(block, jnp.arange(s // Q_BLOCK))
    return out.transpose(1, 0, 3, 2, 4).reshape(b, s, SB_WIDTH)


def causal_depthwise_conv(u, w, bias):
    out = lax.conv_general_dilated(
        u, w[:, None, :], window_strides=(1,), padding=[(SSD_CONV - 1, 0)],
        dimension_numbers=('NWC', 'WIO', 'NWC'), feature_group_count=u.shape[-1])
    return out + bias


def ssd_mixer(z, xbc, dt_raw, conv_w, conv_b, dt_bias, a_log, d_skip, norm_w):
    out_dtype = z.dtype
    b, s, _ = z.shape
    nc = s // CHUNK
    G, E, P, N = SSD_GROUPS, SSD_HEADS_PER_GROUP, SSD_HEAD_DIM, SSD_STATE
    f32 = jnp.float32
    xbc = jax.nn.silu(causal_depthwise_conv(xbc.astype(f32), conv_w.astype(f32), conv_b.astype(f32)))
    x = xbc[..., :SSD_WIDTH].reshape(b, nc, CHUNK, G, E, P)
    Bm = xbc[..., SSD_WIDTH:SSD_WIDTH + G * N].reshape(b, nc, CHUNK, G, N)
    Cm = xbc[..., SSD_WIDTH + G * N:].reshape(b, nc, CHUNK, G, N)
    dt = jax.nn.softplus(dt_raw.astype(f32) + dt_bias.astype(f32)).reshape(b, nc, CHUNK, G, E)
    A = -jnp.exp(a_log.astype(f32)).reshape(G, E)
    a = dt * A
    acum = jnp.cumsum(a, axis=2)
    xdt = x * dt[..., None]
    acum_t = jnp.moveaxis(acum, 2, -1)
    seg = acum_t[..., :, None] - acum_t[..., None, :]
    causal = jnp.tril(jnp.ones((CHUNK, CHUNK), dtype=bool))
    decay = jnp.exp(jnp.where(causal, seg, -jnp.inf))
    cb = jnp.einsum('bclgn,bcsgn->bcgls', Cm, Bm)
    y_diag = jnp.einsum('bcgls,bcgels,bcsgep->bclgep', cb, decay, xdt)
    decay_states = jnp.exp(acum[:, :, -1:] - acum)
    states = jnp.einsum('bclgn,bclge,bclgep->bcgepn', Bm, decay_states, xdt)
    chunk_decay = jnp.exp(acum[:, :, -1])

    def step(state, inp):
        c_c, acum_c, st_c, dec_c = inp
        y_off = jnp.einsum('blgn,bgepn,blge->blgep', c_c, state, jnp.exp(acum_c))
        state = state * dec_c[..., None, None] + st_c
        return state, y_off

    state0 = jnp.zeros((b, G, E, P, N), f32)
    _, y_off = lax.scan(step, state0, (jnp.moveaxis(Cm, 1, 0), jnp.moveaxis(acum, 1, 0),
                                       jnp.moveaxis(states, 1, 0), jnp.moveaxis(chunk_decay, 1, 0)))
    y = y_diag + jnp.moveaxis(y_off, 0, 1) + x * d_skip.astype(f32).reshape(G, E)[..., None]
    y = y.reshape(b, s, SSD_WIDTH)
    y = rms_norm(y * jax.nn.silu(z.astype(f32)), norm_w)
    return y.astype(out_dtype)


def hybrid_layer(x, positions, n_mix_pre, n_mix_post, n_ffn_pre, n_ffn_post, w_in, b_gate, ret_gn_w,
                 conv_w, conv_b, dt_bias, a_log, d_skip, ssd_norm_w, w_br_ret, w_br_sb, w_br_ssd,
                 w_out, w_gate, w_up, w_down):
    b, s, d = x.shape
    h = rms_norm(x, n_mix_pre)
    proj = h @ w_in
    split_points = np.cumsum(IN_SIZES)[:-1].tolist()
    rq, rk, rv, rg, sq, sk, sv, sz, sxbc, sdt, gate_logits = jnp.split(proj, split_points, axis=-1)
    y_ret = retention_mixer(rq, rk, rv, rg, positions, ret_gn_w)
    y_sb = stick_breaking_mixer(sq, sk, sv)
    y_ssd = ssd_mixer(sz, sxbc, sdt, conv_w, conv_b, dt_bias, a_log, d_skip, ssd_norm_w)
    gates = jax.nn.sigmoid(gate_logits + b_gate).reshape(b, s, N_BRANCH, d)
    merged = (gates[:, :, 0] * (y_ret @ w_br_ret)
              + gates[:, :, 1] * (y_sb @ w_br_sb)
              + gates[:, :, 2] * (y_ssd @ w_br_ssd))
    x = x + rms_norm(merged @ w_out, n_mix_post)
    h = rms_norm(x, n_ffn_pre)
    f = (jax.nn.silu(h @ w_gate) * (h @ w_up)) @ w_down
    return x + rms_norm(f, n_ffn_post)


def setup_inputs(seed: int = 0) -> dict:
    key = jax.random.key(seed)
    ks = jax.random.split(key, 24)
    f32 = jnp.float32

    def normal(k, shape, scale):
        return jax.random.normal(k, shape, f32) * scale

    def gain(k, shape):
        return 1.0 + 0.02 * jax.random.normal(k, shape, f32)

    dt_init = jnp.exp(jax.random.uniform(ks[10], (DEPTH, SSD_HEADS), f32)
                      * (math.log(0.1) - math.log(0.001)) + math.log(0.001))
    return {
        'x': jax.random.normal(ks[0], (BATCH, SEQ, D_MODEL), f32),
        'positions': jnp.broadcast_to(jnp.arange(SEQ, dtype=jnp.int32), (BATCH, SEQ)),
        'norm_mix_pre': gain(ks[1], (DEPTH, D_MODEL)),
        'norm_mix_post': gain(ks[2], (DEPTH, D_MODEL)),
        'norm_ffn_pre': gain(ks[3], (DEPTH, D_MODEL)),
        'norm_ffn_post': gain(ks[4], (DEPTH, D_MODEL)),
        'w_in': normal(ks[5], (DEPTH, D_MODEL, IN_COLS), D_MODEL ** -0.5),
        'b_gate': normal(ks[6], (DEPTH, N_BRANCH * D_MODEL), 0.01),
        'ret_gn_w': gain(ks[7], (DEPTH, RET_WIDTH)),
        'ssd_conv_w': normal(ks[8], (DEPTH, SSD_CONV, SSD_CONV_DIM), SSD_CONV ** -0.5),
        'ssd_conv_b': normal(ks[9], (DEPTH, SSD_CONV_DIM), 0.01),
        'ssd_dt_bias': dt_init + jnp.log(-jnp.expm1(-dt_init)),
        'ssd_a_log': jnp.log(jax.random.uniform(ks[11], (DEPTH, SSD_HEADS), f32, 1.0, 16.0)),
        'ssd_d': gain(ks[12], (DEPTH, SSD_HEADS)),
        'ssd_norm_w': gain(ks[13], (DEPTH, SSD_WIDTH)),
        'w_branch_ret': normal(ks[14], (DEPTH, RET_WIDTH, D_MODEL), RET_WIDTH ** -0.5),
        'w_branch_sb': normal(ks[15], (DEPTH, SB_WIDTH, D_MODEL), SB_WIDTH ** -0.5),
        'w_branch_ssd': normal(ks[16], (DEPTH, SSD_WIDTH, D_MODEL), SSD_WIDTH ** -0.5),
        'w_out': normal(ks[17], (DEPTH, D_MODEL, D_MODEL), D_MODEL ** -0.5),
        'ffn_w_gate': normal(ks[18], (DEPTH, D_MODEL, FFN_HIDDEN), D_MODEL ** -0.5),
        'ffn_w_up': normal(ks[19], (DEPTH, D_MODEL, FFN_HIDDEN), D_MODEL ** -0.5),
        'ffn_w_down': normal(ks[20], (DEPTH, FFN_HIDDEN, D_MODEL), FFN_HIDDEN ** -0.5),
    }


def reference(x, positions, norm_mix_pre, norm_mix_post, norm_ffn_pre, norm_ffn_post, w_in, b_gate,
              ret_gn_w, ssd_conv_w, ssd_conv_b, ssd_dt_bias, ssd_a_log, ssd_d, ssd_norm_w,
              w_branch_ret, w_branch_sb, w_branch_ssd, w_out, ffn_w_gate, ffn_w_up, ffn_w_down):
    for l in range(DEPTH):
        x = hybrid_layer(x, positions, norm_mix_pre[l], norm_mix_post[l], norm_ffn_pre[l], norm_ffn_post[l],
                         w_in[l], b_gate[l], ret_gn_w[l], ssd_conv_w[l], ssd_conv_b[l], ssd_dt_bias[l],
                         ssd_a_log[l], ssd_d[l], ssd_norm_w[l], w_branch_ret[l], w_branch_sb[l],
                         w_branch_ssd[l], w_out[l], ffn_w_gate[l], ffn_w_up[l], ffn_w_down[l])
    return x
```

```python
import functools
import math

import jax
import jax.numpy as jnp
from jax import lax
from jax.experimental import pallas as pl
from jax.experimental.pallas import tpu as pltpu

F32 = jnp.float32
BF16 = jnp.bfloat16

NORM_EPS = 1e-6
ROPE_BASE = 10000.0
LANES = 128
HEAD_DIM = 128
RET_CHUNK = 64
RET_BLOCK = 256
SB_BLOCK = 128
SSD_BLOCK = 128
SSD_HEAD_DIM = 64
SSD_GROUPS = 4
SSD_STATE = 128
SSD_CONV = 4
VMEM_LIMIT = 56 * 1024 * 1024


def _params(sem):
    return pltpu.CompilerParams(dimension_semantics=sem, vmem_limit_bytes=VMEM_LIMIT)


def _sigmoid(x):
    return 1.0 / (1.0 + jnp.exp(-x))


def _silu(x):
    return x * _sigmoid(x)


def _softplus(x):
    return jnp.maximum(x, 0.0) + jnp.log1p(jnp.exp(-jnp.abs(x)))


def _split3(x):
    hi = x.astype(BF16)
    r1 = x - hi.astype(F32)
    mid = r1.astype(BF16)
    lo = (r1 - mid.astype(F32)).astype(BF16)
    return hi, mid, lo


def _dot(a, b):
    return jnp.dot(a, b, preferred_element_type=F32)


def _dot_nt(a, b):
    return lax.dot_general(a, b, (((1,), (1,)), ((), ())), preferred_element_type=F32)


def _norm_matmul_kernel(x_ref, g_ref, w_ref, o_ref, h_ref):
    @pl.when(pl.program_id(1) == 0)
    def _():
        x = x_ref[...]
        ms = jnp.mean(x * x, axis=-1, keepdims=True)
        h_ref[...] = (x * lax.rsqrt(ms + NORM_EPS) * g_ref[...]).astype(BF16)

    o_ref[...] = _dot(h_ref[...], w_ref[...]).astype(o_ref.dtype)


def norm_matmul(x, g, w, out_dtype, tm, tn):
    m, d = x.shape
    n = w.shape[1]
    return pl.pallas_call(
        _norm_matmul_kernel,
        grid=(m // tm, n // tn),
        in_specs=[pl.BlockSpec((tm, d), lambda i, j: (i, 0)),
                  pl.BlockSpec((1, d), lambda i, j: (0, 0)),
                  pl.BlockSpec((d, tn), lambda i, j: (0, j))],
        out_specs=pl.BlockSpec((tm, tn), lambda i, j: (i, j)),
        out_shape=jax.ShapeDtypeStruct((m, n), out_dtype),
        scratch_shapes=[pltpu.VMEM((tm, d), BF16)],
        compiler_params=_params(("parallel", "arbitrary")),
        name="norm_matmul",
    )(x, g, w)


def _norm_swiglu_kernel(x_ref, g_ref, wg_ref, wu_ref, o_ref, h_ref):
    @pl.when(pl.program_id(1) == 0)
    def _():
        x = x_ref[...]
        ms = jnp.mean(x * x, axis=-1, keepdims=True)
        h_ref[...] = (x * lax.rsqrt(ms + NORM_EPS) * g_ref[...]).astype(BF16)

    h = h_ref[...]
    o_ref[...] = (_silu(_dot(h, wg_ref[...])) * _dot(h, wu_ref[...])).astype(o_ref.dtype)


def norm_swiglu(x, g, wg, wu, tm, tn):
    m, d = x.shape
    n = wg.shape[1]
    return pl.pallas_call(
        _norm_swiglu_kernel,
        grid=(m // tm, n // tn),
        in_specs=[pl.BlockSpec((tm, d), lambda i, j: (i, 0)),
                  pl.BlockSpec((1, d), lambda i, j: (0, 0)),
                  pl.BlockSpec((d, tn), lambda i, j: (0, j)),
                  pl.BlockSpec((d, tn), lambda i, j: (0, j))],
        out_specs=pl.BlockSpec((tm, tn), lambda i, j: (i, j)),
        out_shape=jax.ShapeDtypeStruct((m, n), BF16),
        scratch_shapes=[pltpu.VMEM((tm, d), BF16)],
        compiler_params=_params(("parallel", "arbitrary")),
        name="norm_swiglu",
    )(x, g, wg, wu)


def _matmul_norm_res_kernel(a_ref, w_ref, x_ref, g_ref, o_ref, acc_ref):
    k = pl.program_id(1)

    @pl.when(k == 0)
    def _():
        acc_ref[...] = jnp.zeros_like(acc_ref)

    acc_ref[...] += _dot(a_ref[...], w_ref[...])

    @pl.when(k == pl.num_programs(1) - 1)
    def _():
        f = acc_ref[...]
        ms = jnp.mean(f * f, axis=-1, keepdims=True)
        o_ref[...] = x_ref[...] + f * lax.rsqrt(ms + NORM_EPS) * g_ref[...]


def matmul_norm_res(a, w, x, g, tm, tk):
    m, kdim = a.shape
    d = w.shape[1]
    return pl.pallas_call(
        _matmul_norm_res_kernel,
        grid=(m // tm, kdim // tk),
        in_specs=[pl.BlockSpec((tm, tk), lambda i, k: (i, k)),
                  pl.BlockSpec((tk, d), lambda i, k: (k, 0)),
                  pl.BlockSpec((tm, d), lambda i, k: (i, 0)),
                  pl.BlockSpec((1, d), lambda i, k: (0, 0))],
        out_specs=pl.BlockSpec((tm, d), lambda i, k: (i, 0)),
        out_shape=jax.ShapeDtypeStruct((m, d), F32),
        scratch_shapes=[pltpu.VMEM((tm, d), F32)],
        compiler_params=_params(("parallel", "arbitrary")),
        name="matmul_norm_res",
    )(a, w, x, g)


def _gated_merge_kernel(yr_ref, ys_ref, yd_ref, wr_ref, ws_ref, wd_ref,
                        gr_ref, gs_ref, gd_ref, br_ref, bs_ref, bd_ref, o_ref):
    def branch(y_ref, w_ref, gl_ref, b_ref):
        gate = _sigmoid(gl_ref[...].astype(F32) + b_ref[...])
        return gate * _dot(y_ref[...], w_ref[...])

    o_ref[...] = (branch(yr_ref, wr_ref, gr_ref, br_ref)
                  + branch(ys_ref, ws_ref, gs_ref, bs_ref)
                  + branch(yd_ref, wd_ref, gd_ref, bd_ref)).astype(o_ref.dtype)


def gated_merge(y_ret, y_sb, y_ssd, w_ret, w_sb, w_ssd, proj, gate_col0, b_gate, tm, tn):
    m, width = y_ret.shape
    d = w_ret.shape[1]
    nb = d // tn
    g0 = gate_col0 // tn
    y_spec = pl.BlockSpec((tm, width), lambda i, j: (i, 0))
    w_spec = pl.BlockSpec((width, tn), lambda i, j: (0, j))

    def gate_spec(b):
        return pl.BlockSpec((tm, tn), lambda i, j: (i, g0 + b * nb + j))

    def bias_spec(b):
        return pl.BlockSpec((1, tn), lambda i, j: (0, b * nb + j))

    return pl.pallas_call(
        _gated_merge_kernel,
        grid=(m // tm, nb),
        in_specs=[y_spec, y_spec, y_spec, w_spec, w_spec, w_spec,
                  gate_spec(0), gate_spec(1), gate_spec(2),
                  bias_spec(0), bias_spec(1), bias_spec(2)],
        out_specs=pl.BlockSpec((tm, tn), lambda i, j: (i, j)),
        out_shape=jax.ShapeDtypeStruct((m, d), BF16),
        compiler_params=_params(("parallel", "arbitrary")),
        name="gated_merge",
    )(y_ret, y_sb, y_ssd, w_ret, w_sb, w_ssd, proj, proj, proj, b_gate, b_gate, b_gate)


def _rotary_kernel(pos_ref, freq_ref, cos_ref, sin_ref):
    ang = pos_ref[...].astype(F32) * freq_ref[...]
    lane = lax.broadcasted_iota(jnp.int32, ang.shape, 1)
    cos_ref[...] = jnp.cos(ang)
    sin_ref[...] = jnp.where(lane < HEAD_DIM // 2, -jnp.sin(ang), jnp.sin(ang))


def rotary_tables(positions_col, ts):
    s = positions_col.shape[0]
    half = HEAD_DIM // 2
    inv_freq = ROPE_BASE ** (-2.0 * jnp.arange(half, dtype=F32) / HEAD_DIM)
    freq = jnp.concatenate([inv_freq, inv_freq]).reshape(1, HEAD_DIM)
    return pl.pallas_call(
        _rotary_kernel,
        grid=(s // ts,),
        in_specs=[pl.BlockSpec((ts, 1), lambda i: (i, 0)),
                  pl.BlockSpec((1, HEAD_DIM), lambda i: (0, 0))],
        out_specs=[pl.BlockSpec((ts, HEAD_DIM), lambda i: (i, 0)),
                   pl.BlockSpec((ts, HEAD_DIM), lambda i: (i, 0))],
        out_shape=[jax.ShapeDtypeStruct((s, HEAD_DIM), F32),
                   jax.ShapeDtypeStruct((s, HEAD_DIM), F32)],
        compiler_params=_params(("parallel",)),
        name="rotary_tables",
    )(positions_col, freq)


def _retention_kernel(q_ref, k_ref, v_ref, g_ref, cos_ref, sin_ref, lg_ref, gn_ref, o_ref,
                      state_ref, decay_ref, qd_ref, kd_ref):
    rb = RET_BLOCK
    lg = lg_ref[...]

    @pl.when(pl.program_id(1) == 0)
    def _():
        state_ref[...] = jnp.zeros_like(state_ref)
        t = lax.broadcasted_iota(jnp.int32, (rb, rb), 0)
        s = lax.broadcasted_iota(jnp.int32, (rb, rb), 1)
        dist = jnp.abs(t - s).astype(F32)
        visible = (s // RET_CHUNK) <= (t // RET_CHUNK)
        decay_ref[...] = jnp.where(visible, jnp.exp(lg[:, :1] * dist), 0.0)
        row = lax.broadcasted_iota(jnp.int32, (rb, HEAD_DIM), 0).astype(F32)
        qd_ref[...] = jnp.exp(lg * (row + 1.0))
        kd_ref[...] = jnp.exp(lg * (rb - 1.0 - row))

    cos = cos_ref[...]
    sin = sin_ref[...]

    def rotate(t):
        return t * cos + pltpu.roll(t, HEAD_DIM // 2, 1) * sin

    q = rotate(q_ref[...].astype(F32))
    k = rotate(k_ref[...].astype(F32)) * (HEAD_DIM ** -0.5)
    v = v_ref[...]
    scores = _dot_nt(q.astype(BF16), k.astype(BF16)) * decay_ref[...]
    o = _dot(scores.astype(BF16), v)
    state = state_ref[...]
    o = o + _dot((q * qd_ref[...]).astype(BF16), state.astype(BF16))
    kdt = jnp.transpose(k * kd_ref[...]).astype(BF16)
    block_decay = jnp.exp(lg * float(rb))
    state_ref[...] = state * block_decay + _dot(kdt, v)

    mu = jnp.mean(o, axis=-1, keepdims=True)
    oc = o - mu
    var = jnp.mean(oc * oc, axis=-1, keepdims=True)
    o = oc * lax.rsqrt(var + NORM_EPS) * gn_ref[...]
    o_ref[...] = (o * _silu(g_ref[...].astype(F32))).astype(o_ref.dtype)


def retention(proj, col0, cos, sin, gn_w, heads):
    s = proj.shape[0]
    rb = RET_BLOCK
    c0 = col0 // HEAD_DIM
    log_gamma = jnp.log1p(-jnp.exp2(-5.0 - jnp.arange(heads, dtype=F32)))
    lg = jnp.broadcast_to(log_gamma[:, None, None], (heads, 1, HEAD_DIM))

    def col_spec(part):
        return pl.BlockSpec((rb, HEAD_DIM), lambda h, r: (r, c0 + part * heads + h))

    tab_spec = pl.BlockSpec((rb, HEAD_DIM), lambda h, r: (r, 0))
    return pl.pallas_call(
        _retention_kernel,
        grid=(heads, s // rb),
        in_specs=[col_spec(0), col_spec(1), col_spec(2), col_spec(3), tab_spec, tab_spec,
                  pl.BlockSpec((None, 1, HEAD_DIM), lambda h, r: (h, 0, 0)),
                  pl.BlockSpec((1, HEAD_DIM), lambda h, r: (0, h))],
        out_specs=pl.BlockSpec((rb, HEAD_DIM), lambda h, r: (r, h)),
        out_shape=jax.ShapeDtypeStruct((s, heads * HEAD_DIM), BF16),
        scratch_shapes=[pltpu.VMEM((HEAD_DIM, HEAD_DIM), F32),
                        pltpu.VMEM((rb, rb), F32),
                        pltpu.VMEM((rb, HEAD_DIM), F32),
                        pltpu.VMEM((rb, HEAD_DIM), F32)],
        compiler_params=_params(("parallel", "arbitrary")),
        name="retention",
    )(proj, proj, proj, proj, cos, sin, lg, gn_w)


def _stickbreak_kernel(q_ref, k_ref, v_ref, o_ref):
    tb = SB_BLOCK
    i = pl.program_id(1)
    q = q_ref[...]
    scale = HEAD_DIM ** -0.5
    row = lax.broadcasted_iota(jnp.int32, (tb, tb), 0)
    col = lax.broadcasted_iota(jnp.int32, (tb, tb), 1)
    suffix_and_total = jnp.concatenate(
        [(row > col).astype(BF16), jnp.ones((tb, tb), BF16)], axis=1)

    def block(jb, masked, acc, carry):
        start = pl.multiple_of(jb * tb, tb)
        kb = k_ref[pl.ds(start, tb), :]
        vb = v_ref[pl.ds(start, tb), :]
        z = _dot_nt(q, kb) * scale
        sp = _softplus(z)
        log_keep = -sp
        if masked:
            mask = col < row
            log_keep = jnp.where(mask, log_keep, 0.0)
        hi = log_keep.astype(BF16)
        lo = (log_keep - hi.astype(F32)).astype(BF16)
        sums = _dot(hi, suffix_and_total) + _dot(lo, suffix_and_total)
        later = sums[:, :tb] + carry
        w = jnp.exp(z - sp + later)
        if masked:
            w = jnp.where(mask, w, 0.0)
        acc = acc + _dot(w.astype(BF16), vb)
        return acc, carry + sums[:, tb:]

    zero = jnp.zeros((tb, HEAD_DIM), F32)
    acc, carry = block(i, True, zero, zero)

    def body(n, ac):
        return block(i - 1 - n, False, *ac)

    acc, carry = lax.fori_loop(0, i, body, (acc, carry))
    o_ref[...] = acc.astype(o_ref.dtype)


def stickbreak(proj, col0, heads):
    s = proj.shape[0]
    tb = SB_BLOCK
    c0 = col0 // HEAD_DIM
    return pl.pallas_call(
        _stickbreak_kernel,
        grid=(heads, s // tb),
        in_specs=[pl.BlockSpec((tb, HEAD_DIM), lambda h, i: (i, c0 + h)),
                  pl.BlockSpec((s, HEAD_DIM), lambda h, i: (0, c0 + heads + h)),
                  pl.BlockSpec((s, HEAD_DIM), lambda h, i: (0, c0 + 2 * heads + h))],
        out_specs=pl.BlockSpec((tb, HEAD_DIM), lambda h, i: (i, h)),
        out_shape=jax.ShapeDtypeStruct((s, heads * HEAD_DIM), BF16),
        compiler_params=_params(("parallel", "arbitrary")),
        name="stickbreak",
    )(proj, proj, proj)


def _ssd_kernel(xbc_ref, z_ref, dt_ref, cw_ref, cb_ref, dtb_ref, alog_ref, dskip_ref, nw_ref, o_ref,
                win_ref, state_ref, y_ref, *, width, heads):
    lb = SSD_BLOCK
    pad = 8
    n_pairs = heads // 2
    heads_per_group = heads // SSD_GROUPS

    @pl.when(pl.program_id(0) == 0)
    def _():
        state_ref[...] = jnp.zeros_like(state_ref)
        win_ref[0:pad, :] = jnp.zeros((pad, win_ref.shape[1]), F32)

    win_ref[pad:pad + lb, :] = xbc_ref[...].astype(F32)
    conv = cb_ref[...]
    for tap in range(SSD_CONV):
        off = pad - (SSD_CONV - 1) + tap
        conv = conv + cw_ref[tap:tap + 1, :] * win_ref[off:off + lb, :]
    win_ref[0:pad, :] = win_ref[lb:lb + pad, :]
    act = _silu(conv)
    xs = act[:, :width]
    gn = SSD_GROUPS * SSD_STATE
    bm = act[:, width:width + gn]
    cm = act[:, width + gn:width + 2 * gn]

    dt = _softplus(dt_ref[...] + dtb_ref[...])
    a = dt * (-jnp.exp(alog_ref[...]))
    row = lax.broadcasted_iota(jnp.int32, (lb, lb), 0)
    col = lax.broadcasted_iota(jnp.int32, (lb, lb), 1)
    causal = col <= row
    tri = causal.astype(BF16)
    a_hi, a_mid, a_lo = _split3(a)
    acum = _dot(tri, a_hi) + _dot(tri, a_mid) + _dot(tri, a_lo)
    acum_t = jnp.transpose(acum)
    lane = lax.broadcasted_iota(jnp.int32, (lb, LANES), 1)
    low_half = lane < SSD_HEAD_DIM

    for g in range(SSD_GROUPS):
        bg = bm[:, g * SSD_STATE:(g + 1) * SSD_STATE]
        cg = cm[:, g * SSD_STATE:(g + 1) * SSD_STATE].astype(BF16)
        cb = _dot_nt(cg, bg.astype(BF16))
        bg_t = jnp.transpose(bg).astype(BF16)
        for pp in range(heads_per_group // 2):
            p = g * (heads_per_group // 2) + pp
            e0, e1 = 2 * p, 2 * p + 1
            m_parts, acol_parts, dt_parts = [], [], []
            for e in (e0, e1):
                acol = jnp.broadcast_to(acum[:, e:e + 1], (lb, lb))
                seg = acol - acum_t[e:e + 1, :]
                decay = jnp.where(causal, jnp.exp(jnp.where(causal, seg, 0.0)), 0.0)
                m_parts.append((cb * decay).astype(BF16))
                acol_parts.append(acol)
                dt_parts.append(jnp.broadcast_to(dt[:, e:e + 1], (lb, LANES)))
            acol_pair = jnp.where(low_half, acol_parts[0], acol_parts[1])
            dt_pair = jnp.where(low_half, dt_parts[0], dt_parts[1])
            x_pair = xs[:, p * LANES:(p + 1) * LANES]
            xdt = x_pair * dt_pair
            xdt_lo = jnp.where(low_half, xdt, 0.0).astype(BF16)
            xdt_hi = jnp.where(low_half, 0.0, xdt).astype(BF16)
            y = _dot(jnp.concatenate(m_parts, axis=1), jnp.concatenate([xdt_lo, xdt_hi], axis=0))
            state = state_ref[p]
            y = y + _dot(cg, state.astype(BF16)) * jnp.exp(acol_pair)
            last = acol_pair[lb - 1:lb, :]
            to_end = jnp.exp(last - acol_pair)
            state_ref[p] = state * jnp.exp(last) + _dot(bg_t, (xdt * to_end).astype(BF16))
            y = y + x_pair * dskip_ref[:, p * LANES:(p + 1) * LANES]
            y_ref[:, p * LANES:(p + 1) * LANES] = y

    yz = y_ref[...] * _silu(z_ref[...].astype(F32))
    ms = jnp.mean(yz * yz, axis=-1, keepdims=True)
    o_ref[...] = (yz * lax.rsqrt(ms + NORM_EPS) * nw_ref[...]).astype(o_ref.dtype)


def ssd(proj, z_col0, xbc_col0, dt_raw, conv_w, conv_b, dt_bias, a_log, d_skip, norm_w, width, heads):
    s = proj.shape[0]
    lb = SSD_BLOCK
    conv_dim = width + 2 * SSD_GROUPS * SSD_STATE
    row_vec = lambda n: pl.BlockSpec((1, n), lambda r: (0, 0))
    kernel = functools.partial(_ssd_kernel, width=width, heads=heads)
    return pl.pallas_call(
        kernel,
        grid=(s // lb,),
        in_specs=[pl.BlockSpec((lb, conv_dim), lambda r: (r, xbc_col0 // conv_dim)),
                  pl.BlockSpec((lb, width), lambda r: (r, z_col0 // width)),
                  pl.BlockSpec((lb, LANES), lambda r: (r, 0)),
                  pl.BlockSpec((SSD_CONV, conv_dim), lambda r: (0, 0)),
                  row_vec(conv_dim), row_vec(LANES), row_vec(LANES), row_vec(width), row_vec(width)],
        out_specs=pl.BlockSpec((lb, width), lambda r: (r, 0)),
        out_shape=jax.ShapeDtypeStruct((s, width), BF16),
        scratch_shapes=[pltpu.VMEM((lb + 8, conv_dim), F32),
                        pltpu.VMEM((heads // 2, SSD_STATE, LANES), F32),
                        pltpu.VMEM((lb, width), F32)],
        compiler_params=_params(("arbitrary",)),
        name="ssd",
    )(proj, proj, dt_raw, conv_w, conv_b, dt_bias, a_log, d_skip, norm_w)


def _pad_lanes(v):
    return jnp.pad(v.astype(F32), (0, LANES - v.shape[0])).reshape(1, LANES)


def kernel(x, positions, norm_mix_pre, norm_mix_post, norm_ffn_pre, norm_ffn_post, w_in, b_gate, ret_gn_w, ssd_conv_w, ssd_conv_b, ssd_dt_bias, ssd_a_log, ssd_d, ssd_norm_w, w_branch_ret, w_branch_sb, w_branch_ssd, w_out, ffn_w_gate, ffn_w_up, ffn_w_down):
    b, s, d = x.shape
    assert b == 1
    depth = w_in.shape[0]
    width = w_branch_ret.shape[1]
    ret_heads = width // HEAD_DIM
    sb_heads = width // HEAD_DIM
    ssd_heads = ssd_d.shape[1]
    conv_dim = ssd_conv_w.shape[2]
    dt_col = 4 * width + 3 * width + width + conv_dim
    ret_col, sb_col = 0, 4 * width
    z_col, xbc_col = 7 * width, 8 * width
    gate_col = dt_col

    xf = x.reshape(s, d)
    cos, sin = rotary_tables(positions.reshape(s, 1), 1024)

    for l in range(depth):
        w_main = jnp.concatenate([w_in[l, :, :dt_col], w_in[l, :, dt_col + ssd_heads:]], axis=1).astype(BF16)
        w_dt = jnp.pad(w_in[l, :, dt_col:dt_col + ssd_heads], ((0, 0), (0, LANES - ssd_heads))).astype(BF16)
        g_pre = norm_mix_pre[l].reshape(1, d)
        proj = norm_matmul(xf, g_pre, w_main, BF16, 1024, 512)
        dt_raw = norm_matmul(xf, g_pre, w_dt, F32, 1024, LANES)

        y_ret = retention(proj, ret_col, cos, sin, ret_gn_w[l].reshape(1, width), ret_heads)
        y_sb = stickbreak(proj, sb_col, sb_heads)
        y_ssd = ssd(proj, z_col, xbc_col, dt_raw, ssd_conv_w[l], ssd_conv_b[l].reshape(1, conv_dim),
                    _pad_lanes(ssd_dt_bias[l]), _pad_lanes(ssd_a_log[l]),
                    jnp.repeat(ssd_d[l], SSD_HEAD_DIM).reshape(1, width),
                    ssd_norm_w[l].reshape(1, width), width, ssd_heads)

        merged = gated_merge(y_ret, y_sb, y_ssd,
                             w_branch_ret[l].astype(BF16), w_branch_sb[l].astype(BF16),
                             w_branch_ssd[l].astype(BF16), proj, gate_col,
                             b_gate[l].reshape(1, 3 * d), 512, 512)
        xf = matmul_norm_res(merged, w_out[l].astype(BF16), xf, norm_mix_post[l].reshape(1, d), 512, 512)

        act = norm_swiglu(xf, norm_ffn_pre[l].reshape(1, d), ffn_w_gate[l].astype(BF16),
                          ffn_w_up[l].astype(BF16), 1024, 512)
        xf = matmul_norm_res(act, ffn_w_down[l].astype(BF16), xf, norm_ffn_post[l].reshape(1, d), 512, 512)

    return xf.reshape(b, s, d)
```

```python
import functools
import math

import jax
import jax.numpy as jnp
from jax import lax
from jax.experimental import pallas as pl
from jax.experimental.pallas import tpu as pltpu

F32 = jnp.float32
BF16 = jnp.bfloat16

NORM_EPS = 1e-6
ROPE_BASE = 10000.0
LOG2E = math.log2(math.e)
LANES = 128
HEAD_DIM = 128
RET_CHUNK = 64
RET_BLOCK = 256
SB_QUERIES = 1024
SB_KEYS = 256
SSD_BLOCK = 128
SSD_HEAD_DIM = 64
SSD_GROUPS = 4
SSD_STATE = 128
SSD_CONV = 4
VMEM_LIMIT = 56 * 1024 * 1024


def _params(sem):
    return pltpu.CompilerParams(dimension_semantics=sem, vmem_limit_bytes=VMEM_LIMIT)


def _sigmoid(x):
    return 1.0 / (1.0 + jnp.exp(-x))


def _silu(x):
    return x * _sigmoid(x)


def _softplus(x):
    return jnp.maximum(x, 0.0) + jnp.log1p(jnp.exp(-jnp.abs(x)))


def _split3(x):
    hi = x.astype(BF16)
    r1 = x - hi.astype(F32)
    mid = r1.astype(BF16)
    lo = (r1 - mid.astype(F32)).astype(BF16)
    return hi, mid, lo


def _dot(a, b):
    return jnp.dot(a, b, preferred_element_type=F32)


def _dot_nt(a, b):
    return lax.dot_general(a, b, (((1,), (1,)), ((), ())), preferred_element_type=F32)


def _norm_matmul_kernel(x_ref, g_ref, w_ref, o_ref, h_ref):
    @pl.when(pl.program_id(1) == 0)
    def _():
        x = x_ref[...]
        ms = jnp.mean(x * x, axis=-1, keepdims=True)
        h_ref[...] = (x * lax.rsqrt(ms + NORM_EPS) * g_ref[...]).astype(BF16)

    o_ref[...] = _dot(h_ref[...], w_ref[...]).astype(o_ref.dtype)


def norm_matmul(x, g, w, out_dtype, tm, tn):
    m, d = x.shape
    n = w.shape[1]
    return pl.pallas_call(
        _norm_matmul_kernel,
        grid=(m // tm, n // tn),
        in_specs=[pl.BlockSpec((tm, d), lambda i, j: (i, 0)),
                  pl.BlockSpec((1, d), lambda i, j: (0, 0)),
                  pl.BlockSpec((d, tn), lambda i, j: (0, j))],
        out_specs=pl.BlockSpec((tm, tn), lambda i, j: (i, j)),
        out_shape=jax.ShapeDtypeStruct((m, n), out_dtype),
        scratch_shapes=[pltpu.VMEM((tm, d), BF16)],
        compiler_params=_params(("parallel", "arbitrary")),
        name="norm_matmul",
    )(x, g, w)


def _norm_swiglu_kernel(x_ref, g_ref, wg_ref, wu_ref, o_ref, h_ref):
    @pl.when(pl.program_id(1) == 0)
    def _():
        x = x_ref[...]
        ms = jnp.mean(x * x, axis=-1, keepdims=True)
        h_ref[...] = (x * lax.rsqrt(ms + NORM_EPS) * g_ref[...]).astype(BF16)

    h = h_ref[...]
    o_ref[...] = (_silu(_dot(h, wg_ref[...])) * _dot(h, wu_ref[...])).astype(o_ref.dtype)


def norm_swiglu(x, g, wg, wu, tm, tn):
    m, d = x.shape
    n = wg.shape[1]
    return pl.pallas_call(
        _norm_swiglu_kernel,
        grid=(m // tm, n // tn),
        in_specs=[pl.BlockSpec((tm, d), lambda i, j: (i, 0)),
                  pl.BlockSpec((1, d), lambda i, j: (0, 0)),
                  pl.BlockSpec((d, tn), lambda i, j: (0, j)),
                  pl.BlockSpec((d, tn), lambda i, j: (0, j))],
        out_specs=pl.BlockSpec((tm, tn), lambda i, j: (i, j)),
        out_shape=jax.ShapeDtypeStruct((m, n), BF16),
        scratch_shapes=[pltpu.VMEM((tm, d), BF16)],
        compiler_params=_params(("parallel", "arbitrary")),
        name="norm_swiglu",
    )(x, g, wg, wu)


def _matmul_norm_res_kernel(a_ref, w_ref, x_ref, g_ref, o_ref, acc_ref):
    k = pl.program_id(1)

    @pl.when(k == 0)
    def _():
        acc_ref[...] = jnp.zeros_like(acc_ref)

    acc_ref[...] += _dot(a_ref[...], w_ref[...])

    @pl.when(k == pl.num_programs(1) - 1)
    def _():
        f = acc_ref[...]
        ms = jnp.mean(f * f, axis=-1, keepdims=True)
        o_ref[...] = x_ref[...] + f * lax.rsqrt(ms + NORM_EPS) * g_ref[...]


def matmul_norm_res(a, w, x, g, tm, tk):
    m, kdim = a.shape
    d = w.shape[1]
    return pl.pallas_call(
        _matmul_norm_res_kernel,
        grid=(m // tm, kdim // tk),
        in_specs=[pl.BlockSpec((tm, tk), lambda i, k: (i, k)),
                  pl.BlockSpec((tk, d), lambda i, k: (k, 0)),
                  pl.BlockSpec((tm, d), lambda i, k: (i, 0)),
                  pl.BlockSpec((1, d), lambda i, k: (0, 0))],
        out_specs=pl.BlockSpec((tm, d), lambda i, k: (i, 0)),
        out_shape=jax.ShapeDtypeStruct((m, d), F32),
        scratch_shapes=[pltpu.VMEM((tm, d), F32)],
        compiler_params=_params(("parallel", "arbitrary")),
        name="matmul_norm_res",
    )(a, w, x, g)


def _gated_merge_kernel(yr_ref, ys_ref, yd_ref, wr_ref, ws_ref, wd_ref,
                        gr_ref, gs_ref, gd_ref, br_ref, bs_ref, bd_ref, o_ref):
    def branch(y_ref, w_ref, gl_ref, b_ref):
        gate = _sigmoid(gl_ref[...].astype(F32) + b_ref[...])
        return gate * _dot(y_ref[...], w_ref[...])

    o_ref[...] = (branch(yr_ref, wr_ref, gr_ref, br_ref)
                  + branch(ys_ref, ws_ref, gs_ref, bs_ref)
                  + branch(yd_ref, wd_ref, gd_ref, bd_ref)).astype(o_ref.dtype)


def gated_merge(y_ret, y_sb, y_ssd, w_ret, w_sb, w_ssd, proj, gate_col0, b_gate, tm, tn):
    m, width = y_ret.shape
    d = w_ret.shape[1]
    nb = d // tn
    g0 = gate_col0 // tn
    y_spec = pl.BlockSpec((tm, width), lambda i, j: (i, 0))
    w_spec = pl.BlockSpec((width, tn), lambda i, j: (0, j))

    def gate_spec(b):
        return pl.BlockSpec((tm, tn), lambda i, j: (i, g0 + b * nb + j))

    def bias_spec(b):
        return pl.BlockSpec((1, tn), lambda i, j: (0, b * nb + j))

    return pl.pallas_call(
        _gated_merge_kernel,
        grid=(m // tm, nb),
        in_specs=[y_spec, y_spec, y_spec, w_spec, w_spec, w_spec,
                  gate_spec(0), gate_spec(1), gate_spec(2),
                  bias_spec(0), bias_spec(1), bias_spec(2)],
        out_specs=pl.BlockSpec((tm, tn), lambda i, j: (i, j)),
        out_shape=jax.ShapeDtypeStruct((m, d), BF16),
        compiler_params=_params(("parallel", "arbitrary")),
        name="gated_merge",
    )(y_ret, y_sb, y_ssd, w_ret, w_sb, w_ssd, proj, proj, proj, b_gate, b_gate, b_gate)


def _rotary_kernel(pos_ref, freq_ref, cos_ref, sin_ref):
    ang = pos_ref[...].astype(F32) * freq_ref[...]
    lane = lax.broadcasted_iota(jnp.int32, ang.shape, 1)
    cos_ref[...] = jnp.cos(ang)
    sin_ref[...] = jnp.where(lane < HEAD_DIM // 2, -jnp.sin(ang), jnp.sin(ang))


def rotary_tables(positions_col, ts):
    s = positions_col.shape[0]
    half = HEAD_DIM // 2
    inv_freq = ROPE_BASE ** (-2.0 * jnp.arange(half, dtype=F32) / HEAD_DIM)
    freq = jnp.concatenate([inv_freq, inv_freq]).reshape(1, HEAD_DIM)
    return pl.pallas_call(
        _rotary_kernel,
        grid=(s // ts,),
        in_specs=[pl.BlockSpec((ts, 1), lambda i: (i, 0)),
                  pl.BlockSpec((1, HEAD_DIM), lambda i: (0, 0))],
        out_specs=[pl.BlockSpec((ts, HEAD_DIM), lambda i: (i, 0)),
                   pl.BlockSpec((ts, HEAD_DIM), lambda i: (i, 0))],
        out_shape=[jax.ShapeDtypeStruct((s, HEAD_DIM), F32),
                   jax.ShapeDtypeStruct((s, HEAD_DIM), F32)],
        compiler_params=_params(("parallel",)),
        name="rotary_tables",
    )(positions_col, freq)


def _retention_kernel(q_ref, k_ref, v_ref, g_ref, cos_ref, sin_ref, lg_ref, gn_ref, o_ref,
                      state_ref, decay_ref, qd_ref, kd_ref):
    rb = RET_BLOCK
    lg = lg_ref[...]

    @pl.when(pl.program_id(1) == 0)
    def _():
        state_ref[...] = jnp.zeros_like(state_ref)
        t = lax.broadcasted_iota(jnp.int32, (rb, rb), 0)
        s = lax.broadcasted_iota(jnp.int32, (rb, rb), 1)
        dist = jnp.abs(t - s).astype(F32)
        visible = (s // RET_CHUNK) <= (t // RET_CHUNK)
        decay_ref[...] = jnp.where(visible, jnp.exp(lg[:, :1] * dist), 0.0)
        row = lax.broadcasted_iota(jnp.int32, (rb, HEAD_DIM), 0).astype(F32)
        qd_ref[...] = jnp.exp(lg * (row + 1.0))
        kd_ref[...] = jnp.exp(lg * (rb - 1.0 - row))

    cos = cos_ref[...]
    sin = sin_ref[...]

    def rotate(t):
        return t * cos + pltpu.roll(t, HEAD_DIM // 2, 1) * sin

    q = rotate(q_ref[...].astype(F32))
    k = rotate(k_ref[...].astype(F32)) * (HEAD_DIM ** -0.5)
    v = v_ref[...]
    scores = _dot_nt(q.astype(BF16), k.astype(BF16)) * decay_ref[...]
    o = _dot(scores.astype(BF16), v)
    state = state_ref[...]
    o = o + _dot((q * qd_ref[...]).astype(BF16), state.astype(BF16))
    kdt = jnp.transpose(k * kd_ref[...]).astype(BF16)
    block_decay = jnp.exp(lg * float(rb))
    state_ref[...] = state * block_decay + _dot(kdt, v)

    mu = jnp.mean(o, axis=-1, keepdims=True)
    oc = o - mu
    var = jnp.mean(oc * oc, axis=-1, keepdims=True)
    o = oc * lax.rsqrt(var + NORM_EPS) * gn_ref[...]
    o_ref[...] = (o * _silu(g_ref[...].astype(F32))).astype(o_ref.dtype)


def retention(proj, col0, cos, sin, gn_w, heads):
    s = proj.shape[0]
    rb = RET_BLOCK
    c0 = col0 // HEAD_DIM
    log_gamma = jnp.log1p(-jnp.exp2(-5.0 - jnp.arange(heads, dtype=F32)))
    lg = jnp.broadcast_to(log_gamma[:, None, None], (heads, 1, HEAD_DIM))

    def col_spec(part):
        return pl.BlockSpec((rb, HEAD_DIM), lambda h, r: (r, c0 + part * heads + h))

    tab_spec = pl.BlockSpec((rb, HEAD_DIM), lambda h, r: (r, 0))
    return pl.pallas_call(
        _retention_kernel,
        grid=(heads, s // rb),
        in_specs=[col_spec(0), col_spec(1), col_spec(2), col_spec(3), tab_spec, tab_spec,
                  pl.BlockSpec((None, 1, HEAD_DIM), lambda h, r: (h, 0, 0)),
                  pl.BlockSpec((1, HEAD_DIM), lambda h, r: (0, h))],
        out_specs=pl.BlockSpec((rb, HEAD_DIM), lambda h, r: (r, h)),
        out_shape=jax.ShapeDtypeStruct((s, heads * HEAD_DIM), BF16),
        scratch_shapes=[pltpu.VMEM((HEAD_DIM, HEAD_DIM), F32),
                        pltpu.VMEM((rb, rb), F32),
                        pltpu.VMEM((rb, HEAD_DIM), F32),
                        pltpu.VMEM((rb, HEAD_DIM), F32)],
        compiler_params=_params(("parallel", "arbitrary")),
        name="retention",
    )(proj, proj, proj, proj, cos, sin, lg, gn_w)


def _stickbreak_kernel(q_ref, k_ref, v_ref, o_ref, acc_ref, carry_ref, arg_ref, total_ref, *, tq):
    tk = SB_KEYS
    i = pl.program_id(1)
    scale = HEAD_DIM ** -0.5
    krow = lax.broadcasted_iota(jnp.int32, (tk, tk), 0)
    kcol = lax.broadcasted_iota(jnp.int32, (tk, tk), 1)
    neg_suffix = jnp.where(krow >= kcol, -1.0, 0.0).astype(BF16)

    acc_ref[...] = jnp.zeros_like(acc_ref)
    carry_ref[...] = jnp.zeros_like(carry_ref)

    def key_rows(key_block):
        return pl.ds(pl.multiple_of(key_block * tk, tk), tk)

    def scores(key_block, row0, mask):
        z = _dot_nt(q_ref[row0:tq, :], k_ref[key_rows(key_block), :]) * scale
        sp = jnp.maximum(z, 0.0) + jnp.log(1.0 + jnp.exp2(jnp.abs(z) * -LOG2E))
        if mask is not None:
            sp = jnp.where(mask, sp, 0.0)
        hi = sp.astype(BF16)
        lo = (sp - hi.astype(F32)).astype(BF16)
        suffix = _dot(hi, neg_suffix) + _dot(lo, neg_suffix)
        return z + suffix, jnp.broadcast_to(suffix[:, :1], (tq - row0, LANES))

    def attend(key_block, row0, mask, arg, total):
        carry = carry_ref[row0:tq, :]
        w = jnp.exp(arg + jnp.concatenate([carry] * (tk // LANES), axis=1))
        if mask is not None:
            w = jnp.where(mask, w, 0.0)
        acc_ref[row0:tq, :] += _dot(w.astype(BF16), v_ref[key_rows(key_block), :])
        carry_ref[row0:tq, :] = carry + total

    bands = tq // tk
    for b in range(bands - 1, -1, -1):
        rows = tq - b * tk
        mask = (lax.broadcasted_iota(jnp.int32, (rows, tk), 1)
                < lax.broadcasted_iota(jnp.int32, (rows, tk), 0))
        attend(i * bands + b, b * tk, mask, *scores(i * bands + b, b * tk, mask))

    n_full = i * bands

    @pl.when(n_full > 0)
    def _():
        arg, total = scores(n_full - 1, 0, None)
        arg_ref[...] = arg
        total_ref[...] = total

        def body(n, c):
            pending = n_full - 1 - n
            arg, total = scores(pending - 1, 0, None)
            attend(pending, 0, None, arg_ref[...], total_ref[...])
            arg_ref[...] = arg
            total_ref[...] = total
            return c

        lax.fori_loop(0, n_full - 1, body, 0)
        attend(0, 0, None, arg_ref[...], total_ref[...])

    o_ref[...] = acc_ref[...].astype(o_ref.dtype)


def stickbreak(proj, col0, heads):
    s = proj.shape[0]
    tq = min(SB_QUERIES, s)
    c0 = col0 // HEAD_DIM
    return pl.pallas_call(
        functools.partial(_stickbreak_kernel, tq=tq),
        grid=(heads, s // tq),
        in_specs=[pl.BlockSpec((tq, HEAD_DIM), lambda h, i: (i, c0 + h)),
                  pl.BlockSpec((s, HEAD_DIM), lambda h, i: (0, c0 + heads + h)),
                  pl.BlockSpec((s, HEAD_DIM), lambda h, i: (0, c0 + 2 * heads + h))],
        out_specs=pl.BlockSpec((tq, HEAD_DIM), lambda h, i: (i, h)),
        out_shape=jax.ShapeDtypeStruct((s, heads * HEAD_DIM), BF16),
        scratch_shapes=[pltpu.VMEM((tq, HEAD_DIM), F32), pltpu.VMEM((tq, LANES), F32),
                        pltpu.VMEM((tq, SB_KEYS), F32), pltpu.VMEM((tq, LANES), F32)],
        compiler_params=_params(("parallel", "arbitrary")),
        name="stickbreak",
    )(proj, proj, proj)


def _ssd_kernel(xbc_ref, z_ref, dt_ref, cw_ref, cb_ref, dtb_ref, alog_ref, dskip_ref, nw_ref, o_ref,
                win_ref, state_ref, y_ref, *, width, heads):
    lb = SSD_BLOCK
    pad = 8
    n_pairs = heads // 2
    heads_per_group = heads // SSD_GROUPS

    @pl.when(pl.program_id(0) == 0)
    def _():
        state_ref[...] = jnp.zeros_like(state_ref)
        win_ref[0:pad, :] = jnp.zeros((pad, win_ref.shape[1]), F32)

    win_ref[pad:pad + lb, :] = xbc_ref[...].astype(F32)
    conv = cb_ref[...]
    for tap in range(SSD_CONV):
        off = pad - (SSD_CONV - 1) + tap
        conv = conv + cw_ref[tap:tap + 1, :] * win_ref[off:off + lb, :]
    win_ref[0:pad, :] = win_ref[lb:lb + pad, :]
    act = _silu(conv)
    xs = act[:, :width]
    gn = SSD_GROUPS * SSD_STATE
    bm = act[:, width:width + gn]
    cm = act[:, width + gn:width + 2 * gn]

    dt = _softplus(dt_ref[...] + dtb_ref[...])
    a = dt * (-jnp.exp(alog_ref[...]))
    row = lax.broadcasted_iota(jnp.int32, (lb, lb), 0)
    col = lax.broadcasted_iota(jnp.int32, (lb, lb), 1)
    causal = col <= row
    tri = causal.astype(BF16)
    a_hi, a_mid, a_lo = _split3(a)
    acum = _dot(tri, a_hi) + _dot(tri, a_mid) + _dot(tri, a_lo)
    acum_t = jnp.transpose(acum)
    lane = lax.broadcasted_iota(jnp.int32, (lb, LANES), 1)
    low_half = lane < SSD_HEAD_DIM

    for g in range(SSD_GROUPS):
        bg = bm[:, g * SSD_STATE:(g + 1) * SSD_STATE]
        cg = cm[:, g * SSD_STATE:(g + 1) * SSD_STATE].astype(BF16)
        cb = _dot_nt(cg, bg.astype(BF16))
        bg_t = jnp.transpose(bg).astype(BF16)
        for pp in range(heads_per_group // 2):
            p = g * (heads_per_group // 2) + pp
            e0, e1 = 2 * p, 2 * p + 1
            m_parts, acol_parts, dt_parts = [], [], []
            for e in (e0, e1):
                acol = jnp.broadcast_to(acum[:, e:e + 1], (lb, lb))
                seg = acol - acum_t[e:e + 1, :]
                decay = jnp.where(causal, jnp.exp(jnp.where(causal, seg, 0.0)), 0.0)
                m_parts.append((cb * decay).astype(BF16))
                acol_parts.append(acol)
                dt_parts.append(jnp.broadcast_to(dt[:, e:e + 1], (lb, LANES)))
            acol_pair = jnp.where(low_half, acol_parts[0], acol_parts[1])
            dt_pair = jnp.where(low_half, dt_parts[0], dt_parts[1])
            x_pair = xs[:, p * LANES:(p + 1) * LANES]
            xdt = x_pair * dt_pair
            xdt_lo = jnp.where(low_half, xdt, 0.0).astype(BF16)
            xdt_hi = jnp.where(low_half, 0.0, xdt).astype(BF16)
            y = _dot(jnp.concatenate(m_parts, axis=1), jnp.concatenate([xdt_lo, xdt_hi], axis=0))
            state = state_ref[p]
            y = y + _dot(cg, state.astype(BF16)) * jnp.exp(acol_pair)
            last = acol_pair[lb - 1:lb, :]
            to_end = jnp.exp(last - acol_pair)
            state_ref[p] = state * jnp.exp(last) + _dot(bg_t, (xdt * to_end).astype(BF16))
            y = y + x_pair * dskip_ref[:, p * LANES:(p + 1) * LANES]
            y_ref[:, p * LANES:(p + 1) * LANES] = y

    yz = y_ref[...] * _silu(z_ref[...].astype(F32))
    ms = jnp.mean(yz * yz, axis=-1, keepdims=True)
    o_ref[...] = (yz * lax.rsqrt(ms + NORM_EPS) * nw_ref[...]).astype(o_ref.dtype)


def ssd(proj, z_col0, xbc_col0, dt_raw, conv_w, conv_b, dt_bias, a_log, d_skip, norm_w, width, heads):
    s = proj.shape[0]
    lb = SSD_BLOCK
    conv_dim = width + 2 * SSD_GROUPS * SSD_STATE
    row_vec = lambda n: pl.BlockSpec((1, n), lambda r: (0, 0))
    kernel = functools.partial(_ssd_kernel, width=width, heads=heads)
    return pl.pallas_call(
        kernel,
        grid=(s // lb,),
        in_specs=[pl.BlockSpec((lb, conv_dim), lambda r: (r, xbc_col0 // conv_dim)),
                  pl.BlockSpec((lb, width), lambda r: (r, z_col0 // width)),
                  pl.BlockSpec((lb, LANES), lambda r: (r, 0)),
                  pl.BlockSpec((SSD_CONV, conv_dim), lambda r: (0, 0)),
                  row_vec(conv_dim), row_vec(LANES), row_vec(LANES), row_vec(width), row_vec(width)],
        out_specs=pl.BlockSpec((lb, width), lambda r: (r, 0)),
        out_shape=jax.ShapeDtypeStruct((s, width), BF16),
        scratch_shapes=[pltpu.VMEM((lb + 8, conv_dim), F32),
                        pltpu.VMEM((heads // 2, SSD_STATE, LANES), F32),
                        pltpu.VMEM((lb, width), F32)],
        compiler_params=_params(("arbitrary",)),
        name="ssd",
    )(proj, proj, dt_raw, conv_w, conv_b, dt_bias, a_log, d_skip, norm_w)


def _pad_lanes(v):
    return jnp.pad(v.astype(F32), (0, LANES - v.shape[0])).reshape(1, LANES)


def kernel(x, positions, norm_mix_pre, norm_mix_post, norm_ffn_pre, norm_ffn_post, w_in, b_gate, ret_gn_w, ssd_conv_w, ssd_conv_b, ssd_dt_bias, ssd_a_log, ssd_d, ssd_norm_w, w_branch_ret, w_branch_sb, w_branch_ssd, w_out, ffn_w_gate, ffn_w_up, ffn_w_down):
    b, s, d = x.shape
    assert b == 1
    depth = w_in.shape[0]
    width = w_branch_ret.shape[1]
    ret_heads = width // HEAD_DIM
    sb_heads = width // HEAD_DIM
    ssd_heads = ssd_d.shape[1]
    conv_dim = ssd_conv_w.shape[2]
    dt_col = 4 * width + 3 * width + width + conv_dim
    ret_col, sb_col = 0, 4 * width
    z_col, xbc_col = 7 * width, 8 * width
    gate_col = dt_col

    xf = x.reshape(s, d)
    cos, sin = rotary_tables(positions.reshape(s, 1), 1024)

    for l in range(depth):
        w_main = jnp.concatenate([w_in[l, :, :dt_col], w_in[l, :, dt_col + ssd_heads:]], axis=1).astype(BF16)
        w_dt = jnp.pad(w_in[l, :, dt_col:dt_col + ssd_heads], ((0, 0), (0, LANES - ssd_heads))).astype(BF16)
        g_pre = norm_mix_pre[l].reshape(1, d)
        proj = norm_matmul(xf, g_pre, w_main, BF16, 1024, 512)
        dt_raw = norm_matmul(xf, g_pre, w_dt, F32, 1024, LANES)

        y_ret = retention(proj, ret_col, cos, sin, ret_gn_w[l].reshape(1, width), ret_heads)
        y_sb = stickbreak(proj, sb_col, sb_heads)
        y_ssd = ssd(proj, z_col, xbc_col, dt_raw, ssd_conv_w[l], ssd_conv_b[l].reshape(1, conv_dim),
                    _pad_lanes(ssd_dt_bias[l]), _pad_lanes(ssd_a_log[l]),
                    jnp.repeat(ssd_d[l], SSD_HEAD_DIM).reshape(1, width),
                    ssd_norm_w[l].reshape(1, width), width, ssd_heads)

        merged = gated_merge(y_ret, y_sb, y_ssd,
                             w_branch_ret[l].astype(BF16), w_branch_sb[l].astype(BF16),
                             w_branch_ssd[l].astype(BF16), proj, gate_col,
                             b_gate[l].reshape(1, 3 * d), 512, 512)
        xf = matmul_norm_res(merged, w_out[l].astype(BF16), xf, norm_mix_post[l].reshape(1, d), 512, 512)

        act = norm_swiglu(xf, norm_ffn_pre[l].reshape(1, d), ffn_w_gate[l].astype(BF16),
                          ffn_w_up[l].astype(BF16), 1024, 512)
        xf = matmul_norm_res(act, ffn_w_down[l].astype(BF16), xf, norm_ffn_post[l].reshape(1, d), 512, 512)

    return xf.reshape(b, s, d)
```

```python
import functools
import math

import jax
import jax.numpy as jnp
from jax import lax
from jax.experimental import pallas as pl
from jax.experimental.pallas import tpu as pltpu

F32 = jnp.float32
BF16 = jnp.bfloat16

NORM_EPS = 1e-6
ROPE_BASE = 10000.0
LOG2E = math.log2(math.e)
LANES = 128
HEAD_DIM = 128
RET_CHUNK = 64
RET_BLOCK = 256
SB_QUERIES = 1024
SB_KEYS = 256
SB_DEAD = -104.0
SSD_BLOCK = 128
SSD_HEAD_DIM = 64
SSD_GROUPS = 4
SSD_STATE = 128
SSD_CONV = 4
VMEM_LIMIT = 56 * 1024 * 1024


def _params(sem):
    return pltpu.CompilerParams(dimension_semantics=sem, vmem_limit_bytes=VMEM_LIMIT)


def _sigmoid(x):
    return 1.0 / (1.0 + jnp.exp(-x))


def _silu(x):
    return x * _sigmoid(x)


def _log1p(e):
    u = 1.0 + e
    d = u - 1.0
    return jnp.where(d == 0.0, e, jnp.log(u) * (e / jnp.where(d == 0.0, 1.0, d)))


def _softplus(x):
    return jnp.maximum(x, 0.0) + _log1p(jnp.exp(-jnp.abs(x)))


def _split3(x):
    hi = x.astype(BF16)
    r1 = x - hi.astype(F32)
    mid = r1.astype(BF16)
    lo = (r1 - mid.astype(F32)).astype(BF16)
    return hi, mid, lo


def _dot(a, b):
    return jnp.dot(a, b, preferred_element_type=F32)


def _dot_nt(a, b):
    return lax.dot_general(a, b, (((1,), (1,)), ((), ())), preferred_element_type=F32)


def _rms_normed(x_ref, g_ref):
    x = x_ref[...]
    ms = jnp.mean(x * x, axis=-1, keepdims=True)
    return (x * lax.rsqrt(ms + NORM_EPS) * g_ref[...]).astype(BF16)


def _norm_matmul_kernel(x_ref, g_ref, w_ref, o_ref, h_ref):
    @pl.when(pl.program_id(1) == 0)
    def _():
        h_ref[...] = _rms_normed(x_ref, g_ref)

    o_ref[...] = _dot(h_ref[...], w_ref[...].astype(BF16)).astype(o_ref.dtype)


def norm_matmul(x, g, w_stack, layer, col0, n, out_dtype, tm, tn):
    m, d = x.shape
    cb0 = col0 // tn
    return pl.pallas_call(
        _norm_matmul_kernel,
        grid=(m // tm, n // tn),
        in_specs=[pl.BlockSpec((tm, d), lambda i, j: (i, 0)),
                  pl.BlockSpec((1, d), lambda i, j: (0, 0)),
                  pl.BlockSpec((None, d, tn), lambda i, j: (layer, 0, cb0 + j))],
        out_specs=pl.BlockSpec((tm, tn), lambda i, j: (i, j)),
        out_shape=jax.ShapeDtypeStruct((m, n), out_dtype),
        scratch_shapes=[pltpu.VMEM((tm, d), BF16)],
        compiler_params=_params(("parallel", "arbitrary")),
        name="norm_matmul",
    )(x, g, w_stack)


def _norm_swiglu_kernel(x_ref, g_ref, wg_ref, wu_ref, o_ref, h_ref):
    @pl.when(pl.program_id(1) == 0)
    def _():
        h_ref[...] = _rms_normed(x_ref, g_ref)

    h = h_ref[...]
    gate = _dot(h, wg_ref[...].astype(BF16))
    o_ref[...] = (_silu(gate) * _dot(h, wu_ref[...].astype(BF16))).astype(o_ref.dtype)


def norm_swiglu(x, g, wg_stack, wu_stack, layer, tm, tn):
    m, d = x.shape
    n = wg_stack.shape[2]
    w_spec = pl.BlockSpec((None, d, tn), lambda i, j: (layer, 0, j))
    return pl.pallas_call(
        _norm_swiglu_kernel,
        grid=(m // tm, n // tn),
        in_specs=[pl.BlockSpec((tm, d), lambda i, j: (i, 0)),
                  pl.BlockSpec((1, d), lambda i, j: (0, 0)),
                  w_spec, w_spec],
        out_specs=pl.BlockSpec((tm, tn), lambda i, j: (i, j)),
        out_shape=jax.ShapeDtypeStruct((m, n), BF16),
        scratch_shapes=[pltpu.VMEM((tm, d), BF16)],
        compiler_params=_params(("parallel", "arbitrary")),
        name="norm_swiglu",
    )(x, g, wg_stack, wu_stack)


def _matmul_norm_res_kernel(a_ref, w_ref, x_ref, g_ref, o_ref, f_ref):
    j = pl.program_id(1)
    nj, _, tn = f_ref.shape
    f_ref[j] = _dot(a_ref[...], w_ref[...])

    @pl.when(j == nj - 1)
    def _():
        ss = jnp.zeros((f_ref.shape[1], 1), F32)
        for c in range(nj):
            f = f_ref[c]
            ss = ss + jnp.sum(f * f, axis=-1, keepdims=True)
        inv = lax.rsqrt(ss / (nj * tn) + NORM_EPS)
        for c in range(nj):
            cols = slice(c * tn, (c + 1) * tn)
            o_ref[:, cols] = x_ref[:, cols] + f_ref[c] * inv * g_ref[:, cols]


def matmul_norm_res(a, w_stack, layer, x, g, tm, tn):
    m, kdim = a.shape
    d = w_stack.shape[2]
    return pl.pallas_call(
        _matmul_norm_res_kernel,
        grid=(m // tm, d // tn),
        in_specs=[pl.BlockSpec((tm, kdim), lambda i, j: (i, 0)),
                  pl.BlockSpec((None, kdim, tn), lambda i, j: (layer, 0, j)),
                  pl.BlockSpec((tm, d), lambda i, j: (i, 0)),
                  pl.BlockSpec((1, d), lambda i, j: (0, 0))],
        out_specs=pl.BlockSpec((tm, d), lambda i, j: (i, 0)),
        out_shape=jax.ShapeDtypeStruct((m, d), F32),
        scratch_shapes=[pltpu.VMEM((d // tn, tm, tn), F32)],
        compiler_params=_params(("parallel", "arbitrary")),
        name="matmul_norm_res",
    )(a, w_stack, x, g)


def _gated_merge_kernel(yr_ref, ys_ref, yd_ref, wr_ref, ws_ref, wd_ref,
                        gr_ref, gs_ref, gd_ref, br_ref, bs_ref, bd_ref, o_ref):
    def branch(y_ref, w_ref, gl_ref, b_ref):
        gate = _sigmoid(gl_ref[...].astype(F32) + b_ref[...])
        return gate * _dot(y_ref[...], w_ref[...])

    o_ref[...] = (branch(yr_ref, wr_ref, gr_ref, br_ref)
                  + branch(ys_ref, ws_ref, gs_ref, bs_ref)
                  + branch(yd_ref, wd_ref, gd_ref, bd_ref)).astype(o_ref.dtype)


def gated_merge(y_ret, y_sb, y_ssd, w_ret, w_sb, w_ssd, layer, gate_logits, b_gate, tm, tn):
    m, width = y_ret.shape
    d = w_ret.shape[2]
    nb = d // tn
    y_spec = pl.BlockSpec((tm, width), lambda i, j: (i, 0))
    w_spec = pl.BlockSpec((None, width, tn), lambda i, j: (layer, 0, j))

    def gate_spec(b):
        return pl.BlockSpec((tm, tn), lambda i, j: (i, b * nb + j))

    def bias_spec(b):
        return pl.BlockSpec((1, tn), lambda i, j: (0, b * nb + j))

    return pl.pallas_call(
        _gated_merge_kernel,
        grid=(m // tm, nb),
        in_specs=[y_spec, y_spec, y_spec, w_spec, w_spec, w_spec,
                  gate_spec(0), gate_spec(1), gate_spec(2),
                  bias_spec(0), bias_spec(1), bias_spec(2)],
        out_specs=pl.BlockSpec((tm, tn), lambda i, j: (i, j)),
        out_shape=jax.ShapeDtypeStruct((m, d), BF16),
        compiler_params=_params(("parallel", "arbitrary")),
        name="gated_merge",
    )(y_ret, y_sb, y_ssd, w_ret, w_sb, w_ssd, gate_logits, gate_logits, gate_logits, b_gate, b_gate, b_gate)


def _rotary_kernel(pos_ref, freq_ref, cos_ref, sin_ref):
    ang = pos_ref[...].astype(F32) * freq_ref[...]
    lane = lax.broadcasted_iota(jnp.int32, ang.shape, 1)
    cos_ref[...] = jnp.cos(ang)
    sin_ref[...] = jnp.where(lane < HEAD_DIM // 2, -jnp.sin(ang), jnp.sin(ang))


def rotary_tables(positions_col, ts):
    s = positions_col.shape[0]
    half = HEAD_DIM // 2
    inv_freq = ROPE_BASE ** (-2.0 * jnp.arange(half, dtype=F32) / HEAD_DIM)
    freq = jnp.concatenate([inv_freq, inv_freq]).reshape(1, HEAD_DIM)
    return pl.pallas_call(
        _rotary_kernel,
        grid=(s // ts,),
        in_specs=[pl.BlockSpec((ts, 1), lambda i: (i, 0)),
                  pl.BlockSpec((1, HEAD_DIM), lambda i: (0, 0))],
        out_specs=[pl.BlockSpec((ts, HEAD_DIM), lambda i: (i, 0)),
                   pl.BlockSpec((ts, HEAD_DIM), lambda i: (i, 0))],
        out_shape=[jax.ShapeDtypeStruct((s, HEAD_DIM), F32),
                   jax.ShapeDtypeStruct((s, HEAD_DIM), F32)],
        compiler_params=_params(("parallel",)),
        name="rotary_tables",
    )(positions_col, freq)


def _retention_kernel(q_ref, k_ref, v_ref, g_ref, cos_ref, sin_ref, lg_ref, gn_ref, o_ref,
                      state_ref, decay_ref, qd_ref, kd_ref, *, heads):
    rb = RET_BLOCK

    @pl.when(pl.program_id(0) == 0)
    def _():
        state_ref[...] = jnp.zeros_like(state_ref)
        t = lax.broadcasted_iota(jnp.int32, (rb, rb), 0)
        s = lax.broadcasted_iota(jnp.int32, (rb, rb), 1)
        dist = jnp.abs(t - s).astype(F32)
        visible = (s // RET_CHUNK) <= (t // RET_CHUNK)
        row = lax.broadcasted_iota(jnp.int32, (rb, HEAD_DIM), 0).astype(F32)
        for h in range(heads):
            lg = lg_ref[h]
            decay_ref[h] = jnp.where(visible, jnp.exp(lg[:, :1] * dist), 0.0)
            qd_ref[h] = jnp.exp(lg * (row + 1.0))
            kd_ref[h] = jnp.exp(lg * (rb - 1.0 - row))

    cos = cos_ref[...]
    sin = sin_ref[...]

    def rotate(t):
        return t * cos + pltpu.roll(t, HEAD_DIM // 2, 1) * sin

    for h in range(heads):
        lanes = slice(h * HEAD_DIM, (h + 1) * HEAD_DIM)
        q = rotate(q_ref[:, lanes].astype(F32))
        k = rotate(k_ref[:, lanes].astype(F32)) * (HEAD_DIM ** -0.5)
        v = v_ref[:, lanes]
        scores = _dot_nt(q.astype(BF16), k.astype(BF16)) * decay_ref[h]
        o = _dot(scores.astype(BF16), v)
        state = state_ref[h]
        o = o + _dot((q * qd_ref[h]).astype(BF16), state.astype(BF16))
        kdt = jnp.transpose(k * kd_ref[h]).astype(BF16)
        block_decay = jnp.exp(lg_ref[h] * float(rb))
        state_ref[h] = state * block_decay + _dot(kdt, v)

        mu = jnp.mean(o, axis=-1, keepdims=True)
        oc = o - mu
        var = jnp.mean(oc * oc, axis=-1, keepdims=True)
        o = oc * lax.rsqrt(var + NORM_EPS) * gn_ref[:, lanes]
        o_ref[:, lanes] = (o * _silu(g_ref[:, lanes].astype(F32))).astype(o_ref.dtype)


def retention(proj, col0, cos, sin, gn_w, heads):
    s = proj.shape[0]
    rb = RET_BLOCK
    width = heads * HEAD_DIM
    c0 = col0 // width
    log_gamma = jnp.log1p(-jnp.exp2(-5.0 - jnp.arange(heads, dtype=F32)))
    lg = jnp.broadcast_to(log_gamma[:, None, None], (heads, 1, HEAD_DIM))

    def col_spec(part):
        return pl.BlockSpec((rb, width), lambda r: (r, c0 + part))

    tab_spec = pl.BlockSpec((rb, HEAD_DIM), lambda r: (r, 0))
    return pl.pallas_call(
        functools.partial(_retention_kernel, heads=heads),
        grid=(s // rb,),
        in_specs=[col_spec(0), col_spec(1), col_spec(2), col_spec(3), tab_spec, tab_spec,
                  pl.BlockSpec((heads, 1, HEAD_DIM), lambda r: (0, 0, 0)),
                  pl.BlockSpec((1, width), lambda r: (0, 0))],
        out_specs=pl.BlockSpec((rb, width), lambda r: (r, 0)),
        out_shape=jax.ShapeDtypeStruct((s, width), BF16),
        scratch_shapes=[pltpu.VMEM((heads, HEAD_DIM, HEAD_DIM), F32),
                        pltpu.VMEM((heads, rb, rb), F32),
                        pltpu.VMEM((heads, rb, HEAD_DIM), F32),
                        pltpu.VMEM((heads, rb, HEAD_DIM), F32)],
        compiler_params=_params(("arbitrary",)),
        name="retention",
    )(proj, proj, proj, proj, cos, sin, lg, gn_w)


def _stickbreak_kernel(q_ref, k_ref, v_ref, o_ref, acc_ref, carry_ref, *, tq):
    tk = SB_KEYS
    i = pl.program_id(1)
    scale = HEAD_DIM ** -0.5
    krow = lax.broadcasted_iota(jnp.int32, (tk, tk), 0)
    kcol = lax.broadcasted_iota(jnp.int32, (tk, tk), 1)
    neg_suffix = jnp.where(krow >= kcol, -1.0, 0.0).astype(BF16)

    acc_ref[...] = jnp.zeros_like(acc_ref)
    carry_ref[...] = jnp.zeros_like(carry_ref)

    def tile(key_block, row0, rows, masked):
        keys = pl.ds(pl.multiple_of(key_block * tk, tk), tk)
        z = _dot_nt(q_ref[row0:row0 + rows, :], k_ref[keys, :]) * scale
        sp = jnp.maximum(z, 0.0) + jnp.log(1.0 + jnp.exp2(jnp.abs(z) * -LOG2E))
        if masked:
            mask = (lax.broadcasted_iota(jnp.int32, (rows, tk), 1)
                    < lax.broadcasted_iota(jnp.int32, (rows, tk), 0))
            sp = jnp.where(mask, sp, 0.0)
        hi = sp.astype(BF16)
        lo = (sp - hi.astype(F32)).astype(BF16)
        suffix = _dot(hi, neg_suffix) + _dot(lo, neg_suffix)
        carry = carry_ref[row0:row0 + rows, :]
        w = jnp.exp(z + suffix + jnp.concatenate([carry] * (tk // LANES), axis=1))
        if masked:
            w = jnp.where(mask, w, 0.0)
        acc_ref[row0:row0 + rows, :] += _dot(w.astype(BF16), v_ref[keys, :])
        carry_ref[row0:row0 + rows, :] = carry + jnp.broadcast_to(suffix[:, :1], (rows, LANES))

    def alive(row0, rows):
        return jnp.max(carry_ref[row0:row0 + rows, :]) > SB_DEAD

    bands = tq // tk
    for b in range(bands - 1, -1, -1):
        row0 = b * tk
        near = min(2 * tk, tq - row0)
        tile(i * bands + b, row0, near, True)
        far0 = row0 + near
        if far0 < tq:
            @pl.when(alive(far0, tq - far0))
            def _():
                tile(i * bands + b, far0, tq - far0, False)

    n_full = i * bands

    def body(n):
        tile(n_full - 1 - n, 0, tq, False)
        return n + 1

    lax.while_loop(lambda n: jnp.logical_and(n < n_full, alive(0, tq)), body, 0)
    o_ref[...] = acc_ref[...].astype(o_ref.dtype)


def stickbreak(proj, col0, heads):
    s = proj.shape[0]
    tq = min(SB_QUERIES, s)
    c0 = col0 // HEAD_DIM
    return pl.pallas_call(
        functools.partial(_stickbreak_kernel, tq=tq),
        grid=(heads, s // tq),
        in_specs=[pl.BlockSpec((tq, HEAD_DIM), lambda h, i: (i, c0 + h)),
                  pl.BlockSpec((s, HEAD_DIM), lambda h, i: (0, c0 + heads + h)),
                  pl.BlockSpec((s, HEAD_DIM), lambda h, i: (0, c0 + 2 * heads + h))],
        out_specs=pl.BlockSpec((tq, HEAD_DIM), lambda h, i: (i, h)),
        out_shape=jax.ShapeDtypeStruct((s, heads * HEAD_DIM), BF16),
        scratch_shapes=[pltpu.VMEM((tq, HEAD_DIM), F32), pltpu.VMEM((tq, LANES), F32)],
        compiler_params=_params(("parallel", "arbitrary")),
        name="stickbreak",
    )(proj, proj, proj)


def _ssd_kernel(xbc_ref, z_ref, dt_ref, cw_ref, cb_ref, dtb_ref, alog_ref, dskip_ref, nw_ref, o_ref,
                win_ref, state_ref, y_ref, *, width, heads):
    lb = SSD_BLOCK
    pad = 8
    heads_per_group = heads // SSD_GROUPS

    @pl.when(pl.program_id(0) == 0)
    def _():
        state_ref[...] = jnp.zeros_like(state_ref)
        win_ref[0:pad, :] = jnp.zeros((pad, win_ref.shape[1]), F32)

    win_ref[pad:pad + lb, :] = xbc_ref[...].astype(F32)
    conv = cb_ref[...]
    for tap in range(SSD_CONV):
        off = pad - (SSD_CONV - 1) + tap
        conv = conv + cw_ref[tap:tap + 1, :] * win_ref[off:off + lb, :]
    win_ref[0:pad, :] = win_ref[lb:lb + pad, :]
    act = _silu(conv)
    xs = act[:, :width]
    gn = SSD_GROUPS * SSD_STATE
    bm = act[:, width:width + gn]
    cm = act[:, width + gn:width + 2 * gn]

    dt = _softplus(dt_ref[...] + dtb_ref[...])
    a = dt * (-jnp.exp(alog_ref[...]))
    row = lax.broadcasted_iota(jnp.int32, (lb, lb), 0)
    col = lax.broadcasted_iota(jnp.int32, (lb, lb), 1)
    causal = col <= row
    tri = jnp.where(causal, 1.0, 0.0).astype(BF16)
    a_hi, a_mid, a_lo = _split3(a)
    acum = _dot(tri, a_hi) + _dot(tri, a_mid) + _dot(tri, a_lo)
    acum_t = jnp.transpose(acum)
    lane = lax.broadcasted_iota(jnp.int32, (lb, LANES), 1)
    low_half = lane < SSD_HEAD_DIM

    for g in range(SSD_GROUPS):
        bg = bm[:, g * SSD_STATE:(g + 1) * SSD_STATE]
        cg = cm[:, g * SSD_STATE:(g + 1) * SSD_STATE].astype(BF16)
        cb = _dot_nt(cg, bg.astype(BF16))
        bg_t = jnp.transpose(bg).astype(BF16)
        for pp in range(heads_per_group // 2):
            p = g * (heads_per_group // 2) + pp
            e0, e1 = 2 * p, 2 * p + 1
            m_parts, acol_parts, dt_parts = [], [], []
            for e in (e0, e1):
                acol = jnp.broadcast_to(acum[:, e:e + 1], (lb, lb))
                seg = acol - acum_t[e:e + 1, :]
                decay = jnp.where(causal, jnp.exp(jnp.where(causal, seg, 0.0)), 0.0)
                m_parts.append((cb * decay).astype(BF16))
                acol_parts.append(acol)
                dt_parts.append(jnp.broadcast_to(dt[:, e:e + 1], (lb, LANES)))
            acol_pair = jnp.where(low_half, acol_parts[0], acol_parts[1])
            dt_pair = jnp.where(low_half, dt_parts[0], dt_parts[1])
            x_pair = xs[:, p * LANES:(p + 1) * LANES]
            xdt = x_pair * dt_pair
            xdt_lo = jnp.where(low_half, xdt, 0.0).astype(BF16)
            xdt_hi = jnp.where(low_half, 0.0, xdt).astype(BF16)
            y = _dot(jnp.concatenate(m_parts, axis=1), jnp.concatenate([xdt_lo, xdt_hi], axis=0))
            state = state_ref[p]
            y = y + _dot(cg, state.astype(BF16)) * jnp.exp(acol_pair)
            last = acol_pair[lb - 1:lb, :]
            to_end = jnp.exp(last - acol_pair)
            state_ref[p] = state * jnp.exp(last) + _dot(bg_t, (xdt * to_end).astype(BF16))
            y = y + x_pair * dskip_ref[:, p * LANES:(p + 1) * LANES]
            y_ref[:, p * LANES:(p + 1) * LANES] = y

    yz = y_ref[...] * _silu(z_ref[...].astype(F32))
    ms = jnp.mean(yz * yz, axis=-1, keepdims=True)
    o_ref[...] = (yz * lax.rsqrt(ms + NORM_EPS) * nw_ref[...]).astype(o_ref.dtype)


def ssd(proj, z_col0, xbc_col0, dt_raw, conv_w, conv_b, dt_bias, a_log, d_skip, norm_w, width, heads):
    s = proj.shape[0]
    lb = SSD_BLOCK
    conv_dim = width + 2 * SSD_GROUPS * SSD_STATE
    row_vec = lambda n: pl.BlockSpec((1, n), lambda r: (0, 0))
    kernel = functools.partial(_ssd_kernel, width=width, heads=heads)
    return pl.pallas_call(
        kernel,
        grid=(s // lb,),
        in_specs=[pl.BlockSpec((lb, conv_dim), lambda r: (r, xbc_col0 // conv_dim)),
                  pl.BlockSpec((lb, width), lambda r: (r, z_col0 // width)),
                  pl.BlockSpec((lb, LANES), lambda r: (r, 0)),
                  pl.BlockSpec((SSD_CONV, conv_dim), lambda r: (0, 0)),
                  row_vec(conv_dim), row_vec(LANES), row_vec(LANES), row_vec(width), row_vec(width)],
        out_specs=pl.BlockSpec((lb, width), lambda r: (r, 0)),
        out_shape=jax.ShapeDtypeStruct((s, width), BF16),
        scratch_shapes=[pltpu.VMEM((lb + 8, conv_dim), F32),
                        pltpu.VMEM((heads // 2, SSD_STATE, LANES), F32),
                        pltpu.VMEM((lb, width), F32)],
        compiler_params=_params(("arbitrary",)),
        name="ssd",
    )(proj, proj, dt_raw, conv_w, conv_b, dt_bias, a_log, d_skip, norm_w)


def _pad_lanes(v):
    return jnp.pad(v.astype(F32), (0, LANES - v.shape[0])).reshape(1, LANES)


def kernel(x, positions, norm_mix_pre, norm_mix_post, norm_ffn_pre, norm_ffn_post, w_in, b_gate, ret_gn_w, ssd_conv_w, ssd_conv_b, ssd_dt_bias, ssd_a_log, ssd_d, ssd_norm_w, w_branch_ret, w_branch_sb, w_branch_ssd, w_out, ffn_w_gate, ffn_w_up, ffn_w_down):
    b, s, d = x.shape
    assert b == 1
    depth = w_in.shape[0]
    width = w_branch_ret.shape[1]
    ret_heads = width // HEAD_DIM
    sb_heads = width // HEAD_DIM
    ssd_heads = ssd_d.shape[1]
    conv_dim = ssd_conv_w.shape[2]
    dt_col = 4 * width + 3 * width + width + conv_dim
    ret_col, sb_col = 0, 4 * width
    z_col, xbc_col = 7 * width, 8 * width
    assert dt_col % LANES == 0 and ssd_heads <= LANES

    w_gates = w_in[:, :, dt_col + ssd_heads:].astype(BF16)
    w_ret, w_sb, w_ssd = (w.astype(BF16) for w in (w_branch_ret, w_branch_sb, w_branch_ssd))
    w_out_b = w_out.astype(BF16)
    w_down_b = ffn_w_down.astype(BF16)

    xf = x.reshape(s, d)
    cos, sin = rotary_tables(positions.reshape(s, 1), 1024)

    for l in range(depth):
        g_pre = norm_mix_pre[l].reshape(1, d)
        proj = norm_matmul(xf, g_pre, w_in, l, 0, dt_col, BF16, 1024, 512)
        dt_raw = norm_matmul(xf, g_pre, w_in, l, dt_col, LANES, F32, 1024, LANES)
        gate_logits = norm_matmul(xf, g_pre, w_gates, l, 0, 3 * d, BF16, 1024, 512)

        y_ret = retention(proj, ret_col, cos, sin, ret_gn_w[l].reshape(1, width), ret_heads)
        y_sb = stickbreak(proj, sb_col, sb_heads)
        y_ssd = ssd(proj, z_col, xbc_col, dt_raw, ssd_conv_w[l], ssd_conv_b[l].reshape(1, conv_dim),
                    _pad_lanes(ssd_dt_bias[l]), _pad_lanes(ssd_a_log[l]),
                    jnp.repeat(ssd_d[l], SSD_HEAD_DIM).reshape(1, width),
                    ssd_norm_w[l].reshape(1, width), width, ssd_heads)

        merged = gated_merge(y_ret, y_sb, y_ssd, w_ret, w_sb, w_ssd, l, gate_logits,
                             b_gate[l].reshape(1, 3 * d), 1024, 512)
        xf = matmul_norm_res(merged, w_out_b, l, xf, norm_mix_post[l].reshape(1, d), 512, 512)

        act = norm_swiglu(xf, norm_ffn_pre[l].reshape(1, d), ffn_w_gate, ffn_w_up, l, 1024, 512)
        xf = matmul_norm_res(act, w_down_b, l, xf, norm_ffn_post[l].reshape(1, d), 512, 512)

    return xf.reshape(b, s, d)
```

```python
import functools
import math

import jax
import jax.numpy as jnp
from jax import lax
from jax.experimental import pallas as pl
from jax.experimental.pallas import tpu as pltpu

F32 = jnp.float32
BF16 = jnp.bfloat16

NORM_EPS = 1e-6
ROPE_BASE = 10000.0
LOG2E = math.log2(math.e)
LANES = 128
SUBLANES = 8
HEAD_DIM = 128
RET_CHUNK = 64
RET_BLOCK = 256
SB_QUERIES = 1024
SB_KEYS = 256
SB_DEAD = -104.0
SSD_BLOCK = 128
SSD_HEAD_DIM = 64
SSD_GROUPS = 4
SSD_STATE = 128
SSD_CONV = 4
VMEM_LIMIT = 56 * 1024 * 1024


def _params(sem):
    return pltpu.CompilerParams(dimension_semantics=sem, vmem_limit_bytes=VMEM_LIMIT)


def _sigmoid(x):
    return 1.0 / (1.0 + jnp.exp(-x))


def _silu(x):
    return x * _sigmoid(x)


def _log1p(e):
    u = 1.0 + e
    d = u - 1.0
    return jnp.where(d == 0.0, e, jnp.log(u) * (e / jnp.where(d == 0.0, 1.0, d)))


def _softplus(x):
    return jnp.maximum(x, 0.0) + _log1p(jnp.exp(-jnp.abs(x)))


def _split3(x):
    hi = x.astype(BF16)
    r1 = x - hi.astype(F32)
    mid = r1.astype(BF16)
    lo = (r1 - mid.astype(F32)).astype(BF16)
    return hi, mid, lo


def _dot(a, b):
    return jnp.dot(a, b, preferred_element_type=F32)


def _dot_nt(a, b):
    return lax.dot_general(a, b, (((1,), (1,)), ((), ())), preferred_element_type=F32)


def _rms_norm_kernel(x_ref, g_ref, h_ref):
    x = x_ref[...]
    ms = jnp.mean(x * x, axis=-1, keepdims=True)
    h_ref[...] = (x * lax.rsqrt(ms + NORM_EPS) * g_ref[...]).astype(h_ref.dtype)


def rms_norm(x, g, tm):
    m, d = x.shape
    return pl.pallas_call(
        _rms_norm_kernel,
        grid=(m // tm,),
        in_specs=[pl.BlockSpec((tm, d), lambda i: (i, 0)), pl.BlockSpec((1, d), lambda i: (0, 0))],
        out_specs=pl.BlockSpec((tm, d), lambda i: (i, 0)),
        out_shape=jax.ShapeDtypeStruct((m, d), BF16),
        compiler_params=_params(("parallel",)),
        name="rms_norm",
    )(x, g)


def _matmul_nt_kernel(h_ref, wt_ref, o_ref):
    o_ref[...] = _dot_nt(h_ref[...], wt_ref[...].astype(BF16)).astype(o_ref.dtype)


def matmul_nt(h, wt, row0, n, out_dtype, tm, tn):
    m, d = h.shape
    assert row0 % SUBLANES == 0 and tn % SUBLANES == 0
    return pl.pallas_call(
        _matmul_nt_kernel,
        grid=(m // tm, n // tn),
        in_specs=[pl.BlockSpec((tm, d), lambda i, j: (i, 0)),
                  pl.BlockSpec((pl.Element(tn), pl.Element(d)),
                               lambda i, j: (pl.multiple_of(row0 + j * tn, SUBLANES), 0))],
        out_specs=pl.BlockSpec((tm, tn), lambda i, j: (i, j)),
        out_shape=jax.ShapeDtypeStruct((m, n), out_dtype),
        compiler_params=_params(("parallel", "arbitrary")),
        name="matmul_nt",
    )(h, wt)


def _swiglu_kernel(h_ref, wg_ref, wu_ref, o_ref):
    h = h_ref[...]
    gate = _dot(h, wg_ref[...].astype(BF16))
    o_ref[...] = (_silu(gate) * _dot(h, wu_ref[...].astype(BF16))).astype(o_ref.dtype)


def swiglu(h, wg_stack, wu_stack, layer, tm, tn):
    m, d = h.shape
    n = wg_stack.shape[2]
    w_spec = pl.BlockSpec((None, d, tn), lambda i, j: (layer, 0, j))
    return pl.pallas_call(
        _swiglu_kernel,
        grid=(m // tm, n // tn),
        in_specs=[pl.BlockSpec((tm, d), lambda i, j: (i, 0)), w_spec, w_spec],
        out_specs=pl.BlockSpec((tm, tn), lambda i, j: (i, j)),
        out_shape=jax.ShapeDtypeStruct((m, n), BF16),
        compiler_params=_params(("parallel", "arbitrary")),
        name="swiglu",
    )(h, wg_stack, wu_stack)


def _matmul_norm_res_kernel(a_ref, w_ref, x_ref, g_ref, gn_ref, o_ref, h_ref, f_ref):
    j = pl.program_id(1)
    nj, _, tn = f_ref.shape
    f_ref[j] = _dot(a_ref[...], w_ref[...])

    @pl.when(j == nj - 1)
    def _():
        ss = jnp.zeros((f_ref.shape[1], 1), F32)
        for c in range(nj):
            f = f_ref[c]
            ss = ss + jnp.sum(f * f, axis=-1, keepdims=True)
        inv = lax.rsqrt(ss / (nj * tn) + NORM_EPS)
        ss = jnp.zeros((f_ref.shape[1], 1), F32)
        for c in range(nj):
            cols = slice(c * tn, (c + 1) * tn)
            y = x_ref[:, cols] + f_ref[c] * inv * g_ref[:, cols]
            o_ref[:, cols] = y
            ss = ss + jnp.sum(y * y, axis=-1, keepdims=True)
        inv = lax.rsqrt(ss / (nj * tn) + NORM_EPS)
        for c in range(nj):
            cols = slice(c * tn, (c + 1) * tn)
            h_ref[:, cols] = (o_ref[:, cols] * inv * gn_ref[:, cols]).astype(h_ref.dtype)


def matmul_norm_res(a, w_stack, layer, x, g, g_next, tm, tn):
    m, kdim = a.shape
    d = w_stack.shape[2]
    row_spec = pl.BlockSpec((tm, d), lambda i, j: (i, 0))
    vec_spec = pl.BlockSpec((1, d), lambda i, j: (0, 0))
    return pl.pallas_call(
        _matmul_norm_res_kernel,
        grid=(m // tm, d // tn),
        in_specs=[pl.BlockSpec((tm, kdim), lambda i, j: (i, 0)),
                  pl.BlockSpec((None, kdim, tn), lambda i, j: (layer, 0, j)),
                  row_spec, vec_spec, vec_spec],
        out_specs=[row_spec, row_spec],
        out_shape=[jax.ShapeDtypeStruct((m, d), F32), jax.ShapeDtypeStruct((m, d), BF16)],
        scratch_shapes=[pltpu.VMEM((d // tn, tm, tn), F32)],
        compiler_params=_params(("parallel", "arbitrary")),
        name="matmul_norm_res",
    )(a, w_stack, x, g, g_next)


def _gated_merge_kernel(yr_ref, ys_ref, yd_ref, wr_ref, ws_ref, wd_ref,
                        gr_ref, gs_ref, gd_ref, br_ref, bs_ref, bd_ref, o_ref):
    def branch(y_ref, w_ref, gl_ref, b_ref):
        gate = _sigmoid(gl_ref[...].astype(F32) + b_ref[...])
        return gate * _dot(y_ref[...], w_ref[...])

    o_ref[...] = (branch(yr_ref, wr_ref, gr_ref, br_ref)
                  + branch(ys_ref, ws_ref, gs_ref, bs_ref)
                  + branch(yd_ref, wd_ref, gd_ref, bd_ref)).astype(o_ref.dtype)


def gated_merge(y_ret, y_sb, y_ssd, w_ret, w_sb, w_ssd, layer, gate_logits, b_gate, tm, tn):
    m, width = y_ret.shape
    d = w_ret.shape[2]
    nb = d // tn
    y_spec = pl.BlockSpec((tm, width), lambda i, j: (i, 0))
    w_spec = pl.BlockSpec((None, width, tn), lambda i, j: (layer, 0, j))

    def gate_spec(b):
        return pl.BlockSpec((tm, tn), lambda i, j: (i, b * nb + j))

    def bias_spec(b):
        return pl.BlockSpec((1, tn), lambda i, j: (0, b * nb + j))

    return pl.pallas_call(
        _gated_merge_kernel,
        grid=(m // tm, nb),
        in_specs=[y_spec, y_spec, y_spec, w_spec, w_spec, w_spec,
                  gate_spec(0), gate_spec(1), gate_spec(2),
                  bias_spec(0), bias_spec(1), bias_spec(2)],
        out_specs=pl.BlockSpec((tm, tn), lambda i, j: (i, j)),
        out_shape=jax.ShapeDtypeStruct((m, d), BF16),
        compiler_params=_params(("parallel", "arbitrary")),
        name="gated_merge",
    )(y_ret, y_sb, y_ssd, w_ret, w_sb, w_ssd, gate_logits, gate_logits, gate_logits, b_gate, b_gate, b_gate)


def _rotary_kernel(pos_ref, freq_ref, cos_ref, sin_ref):
    ang = pos_ref[...].astype(F32) * freq_ref[...]
    lane = lax.broadcasted_iota(jnp.int32, ang.shape, 1)
    cos_ref[...] = jnp.cos(ang)
    sin_ref[...] = jnp.where(lane < HEAD_DIM // 2, -jnp.sin(ang), jnp.sin(ang))


def rotary_tables(positions_col, ts):
    s = positions_col.shape[0]
    half = HEAD_DIM // 2
    inv_freq = ROPE_BASE ** (-2.0 * jnp.arange(half, dtype=F32) / HEAD_DIM)
    freq = jnp.concatenate([inv_freq, inv_freq]).reshape(1, HEAD_DIM)
    return pl.pallas_call(
        _rotary_kernel,
        grid=(s // ts,),
        in_specs=[pl.BlockSpec((ts, 1), lambda i: (i, 0)),
                  pl.BlockSpec((1, HEAD_DIM), lambda i: (0, 0))],
        out_specs=[pl.BlockSpec((ts, HEAD_DIM), lambda i: (i, 0)),
                   pl.BlockSpec((ts, HEAD_DIM), lambda i: (i, 0))],
        out_shape=[jax.ShapeDtypeStruct((s, HEAD_DIM), F32),
                   jax.ShapeDtypeStruct((s, HEAD_DIM), F32)],
        compiler_params=_params(("parallel",)),
        name="rotary_tables",
    )(positions_col, freq)


def _retention_kernel(q_ref, k_ref, v_ref, g_ref, cos_ref, sin_ref, lg_ref, gn_ref, o_ref,
                      state_ref, decay_ref, qd_ref, kd_ref, *, heads):
    rb = RET_BLOCK

    @pl.when(pl.program_id(0) == 0)
    def _():
        state_ref[...] = jnp.zeros_like(state_ref)
        t = lax.broadcasted_iota(jnp.int32, (rb, rb), 0)
        s = lax.broadcasted_iota(jnp.int32, (rb, rb), 1)
        dist = jnp.abs(t - s).astype(F32)
        visible = (s // RET_CHUNK) <= (t // RET_CHUNK)
        row = lax.broadcasted_iota(jnp.int32, (rb, HEAD_DIM), 0).astype(F32)
        for h in range(heads):
            lg = lg_ref[h]
            decay_ref[h] = jnp.where(visible, jnp.exp(lg[:, :1] * dist), 0.0)
            qd_ref[h] = jnp.exp(lg * (row + 1.0))
            kd_ref[h] = jnp.exp(lg * (rb - 1.0 - row))

    cos = cos_ref[...]
    sin = sin_ref[...]

    def rotate(t):
        return t * cos + pltpu.roll(t, HEAD_DIM // 2, 1) * sin

    for h in range(heads):
        lanes = slice(h * HEAD_DIM, (h + 1) * HEAD_DIM)
        q = rotate(q_ref[:, lanes].astype(F32))
        k = rotate(k_ref[:, lanes].astype(F32)) * (HEAD_DIM ** -0.5)
        v = v_ref[:, lanes]
        scores = _dot_nt(q.astype(BF16), k.astype(BF16)) * decay_ref[h]
        o = _dot(scores.astype(BF16), v)
        state = state_ref[h]
        o = o + _dot((q * qd_ref[h]).astype(BF16), state.astype(BF16))
        kdt = jnp.transpose(k * kd_ref[h]).astype(BF16)
        block_decay = jnp.exp(lg_ref[h] * float(rb))
        state_ref[h] = state * block_decay + _dot(kdt, v)

        mu = jnp.mean(o, axis=-1, keepdims=True)
        oc = o - mu
        var = jnp.mean(oc * oc, axis=-1, keepdims=True)
        o = oc * lax.rsqrt(var + NORM_EPS) * gn_ref[:, lanes]
        o_ref[:, lanes] = (o * _silu(g_ref[:, lanes].astype(F32))).astype(o_ref.dtype)


def retention(proj, col0, cos, sin, gn_w, heads):
    s = proj.shape[0]
    rb = RET_BLOCK
    width = heads * HEAD_DIM
    c0 = col0 // width
    log_gamma = jnp.log1p(-jnp.exp2(-5.0 - jnp.arange(heads, dtype=F32)))
    lg = jnp.broadcast_to(log_gamma[:, None, None], (heads, 1, HEAD_DIM))

    def col_spec(part):
        return pl.BlockSpec((rb, width), lambda r: (r, c0 + part))

    tab_spec = pl.BlockSpec((rb, HEAD_DIM), lambda r: (r, 0))
    return pl.pallas_call(
        functools.partial(_retention_kernel, heads=heads),
        grid=(s // rb,),
        in_specs=[col_spec(0), col_spec(1), col_spec(2), col_spec(3), tab_spec, tab_spec,
                  pl.BlockSpec((heads, 1, HEAD_DIM), lambda r: (0, 0, 0)),
                  pl.BlockSpec((1, width), lambda r: (0, 0))],
        out_specs=pl.BlockSpec((rb, width), lambda r: (r, 0)),
        out_shape=jax.ShapeDtypeStruct((s, width), BF16),
        scratch_shapes=[pltpu.VMEM((heads, HEAD_DIM, HEAD_DIM), F32),
                        pltpu.VMEM((heads, rb, rb), F32),
                        pltpu.VMEM((heads, rb, HEAD_DIM), F32),
                        pltpu.VMEM((heads, rb, HEAD_DIM), F32)],
        compiler_params=_params(("arbitrary",)),
        name="retention",
    )(proj, proj, proj, proj, cos, sin, lg, gn_w)


def _stickbreak_kernel(q_ref, k_ref, v_ref, o_ref, acc_ref, carry_ref, *, tq):
    tk = SB_KEYS
    i = pl.program_id(1)
    scale = HEAD_DIM ** -0.5
    krow = lax.broadcasted_iota(jnp.int32, (tk, tk), 0)
    kcol = lax.broadcasted_iota(jnp.int32, (tk, tk), 1)
    neg_suffix = jnp.where(krow >= kcol, -1.0, 0.0).astype(BF16)

    acc_ref[...] = jnp.zeros_like(acc_ref)
    carry_ref[...] = jnp.zeros_like(carry_ref)

    def tile(key_block, row0, rows, masked):
        keys = pl.ds(pl.multiple_of(key_block * tk, tk), tk)
        z = _dot_nt(q_ref[row0:row0 + rows, :], k_ref[keys, :]) * scale
        sp = jnp.maximum(z, 0.0) + jnp.log(1.0 + jnp.exp2(jnp.abs(z) * -LOG2E))
        if masked:
            mask = (lax.broadcasted_iota(jnp.int32, (rows, tk), 1)
                    < lax.broadcasted_iota(jnp.int32, (rows, tk), 0))
            sp = jnp.where(mask, sp, 0.0)
        hi = sp.astype(BF16)
        lo = (sp - hi.astype(F32)).astype(BF16)
        suffix = _dot(hi, neg_suffix) + _dot(lo, neg_suffix)
        carry = carry_ref[row0:row0 + rows, :]
        w = jnp.exp(z + suffix + jnp.concatenate([carry] * (tk // LANES), axis=1))
        if masked:
            w = jnp.where(mask, w, 0.0)
        acc_ref[row0:row0 + rows, :] += _dot(w.astype(BF16), v_ref[keys, :])
        carry_ref[row0:row0 + rows, :] = carry + jnp.broadcast_to(suffix[:, :1], (rows, LANES))

    def alive(row0, rows):
        return jnp.max(carry_ref[row0:row0 + rows, :]) > SB_DEAD

    bands = tq // tk
    for b in range(bands - 1, -1, -1):
        row0 = b * tk
        near = min(2 * tk, tq - row0)
        tile(i * bands + b, row0, near, True)
        far0 = row0 + near
        if far0 < tq:
            @pl.when(alive(far0, tq - far0))
            def _():
                tile(i * bands + b, far0, tq - far0, False)

    n_full = i * bands

    def body(n):
        tile(n_full - 1 - n, 0, tk, False)

        @pl.when(alive(tk, tq - tk))
        def _():
            tile(n_full - 1 - n, tk, tq - tk, False)

        return n + 1

    lax.while_loop(lambda n: jnp.logical_and(n < n_full, alive(0, tq)), body, 0)
    o_ref[...] = acc_ref[...].astype(o_ref.dtype)


def stickbreak(proj, col0, heads):
    s = proj.shape[0]
    tq = min(SB_QUERIES, s)
    c0 = col0 // HEAD_DIM
    return pl.pallas_call(
        functools.partial(_stickbreak_kernel, tq=tq),
        grid=(heads, s // tq),
        in_specs=[pl.BlockSpec((tq, HEAD_DIM), lambda h, i: (i, c0 + h)),
                  pl.BlockSpec((s, HEAD_DIM), lambda h, i: (0, c0 + heads + h)),
                  pl.BlockSpec((s, HEAD_DIM), lambda h, i: (0, c0 + 2 * heads + h))],
        out_specs=pl.BlockSpec((tq, HEAD_DIM), lambda h, i: (i, h)),
        out_shape=jax.ShapeDtypeStruct((s, heads * HEAD_DIM), BF16),
        scratch_shapes=[pltpu.VMEM((tq, HEAD_DIM), F32), pltpu.VMEM((tq, LANES), F32)],
        compiler_params=_params(("parallel", "arbitrary")),
        name="stickbreak",
    )(proj, proj, proj)


def _ssd_kernel(xbc_ref, z_ref, dt_ref, cw_ref, cb_ref, dtb_ref, alog_ref, dskip_ref, nw_ref, o_ref,
                win_ref, state_ref, y_ref, *, width, heads):
    lb = SSD_BLOCK
    pad = 8
    heads_per_group = heads // SSD_GROUPS

    @pl.when(pl.program_id(0) == 0)
    def _():
        state_ref[...] = jnp.zeros_like(state_ref)
        win_ref[0:pad, :] = jnp.zeros((pad, win_ref.shape[1]), F32)

    win_ref[pad:pad + lb, :] = xbc_ref[...].astype(F32)
    conv = cb_ref[...]
    for tap in range(SSD_CONV):
        off = pad - (SSD_CONV - 1) + tap
        conv = conv + cw_ref[tap:tap + 1, :] * win_ref[off:off + lb, :]
    win_ref[0:pad, :] = win_ref[lb:lb + pad, :]
    act = _silu(conv)
    xs = act[:, :width]
    gn = SSD_GROUPS * SSD_STATE
    bm = act[:, width:width + gn]
    cm = act[:, width + gn:width + 2 * gn]

    dt = _softplus(dt_ref[...] + dtb_ref[...])
    a = dt * (-jnp.exp(alog_ref[...]))
    row = lax.broadcasted_iota(jnp.int32, (lb, lb), 0)
    col = lax.broadcasted_iota(jnp.int32, (lb, lb), 1)
    causal = col <= row
    tri = jnp.where(causal, 1.0, 0.0).astype(BF16)
    a_hi, a_mid, a_lo = _split3(a)
    acum = _dot(tri, a_hi) + _dot(tri, a_mid) + _dot(tri, a_lo)
    acum_t = jnp.transpose(acum)
    lane = lax.broadcasted_iota(jnp.int32, (lb, LANES), 1)
    low_half = lane < SSD_HEAD_DIM

    for g in range(SSD_GROUPS):
        bg = bm[:, g * SSD_STATE:(g + 1) * SSD_STATE]
        cg = cm[:, g * SSD_STATE:(g + 1) * SSD_STATE].astype(BF16)
        cb = _dot_nt(cg, bg.astype(BF16))
        bg_t = jnp.transpose(bg).astype(BF16)
        for pp in range(heads_per_group // 2):
            p = g * (heads_per_group // 2) + pp
            e0, e1 = 2 * p, 2 * p + 1
            m_parts, acol_parts, dt_parts = [], [], []
            for e in (e0, e1):
                acol = jnp.broadcast_to(acum[:, e:e + 1], (lb, lb))
                seg = acol - acum_t[e:e + 1, :]
                decay = jnp.where(causal, jnp.exp(jnp.where(causal, seg, 0.0)), 0.0)
                m_parts.append((cb * decay).astype(BF16))
                acol_parts.append(acol)
                dt_parts.append(jnp.broadcast_to(dt[:, e:e + 1], (lb, LANES)))
            acol_pair = jnp.where(low_half, acol_parts[0], acol_parts[1])
            dt_pair = jnp.where(low_half, dt_parts[0], dt_parts[1])
            x_pair = xs[:, p * LANES:(p + 1) * LANES]
            xdt = x_pair * dt_pair
            xdt_lo = jnp.where(low_half, xdt, 0.0).astype(BF16)
            xdt_hi = jnp.where(low_half, 0.0, xdt).astype(BF16)
            y = _dot(jnp.concatenate(m_parts, axis=1), jnp.concatenate([xdt_lo, xdt_hi], axis=0))
            state = state_ref[p]
            y = y + _dot(cg, state.astype(BF16)) * jnp.exp(acol_pair)
            last = acol_pair[lb - 1:lb, :]
            to_end = jnp.exp(last - acol_pair)
            state_ref[p] = state * jnp.exp(last) + _dot(bg_t, (xdt * to_end).astype(BF16))
            y = y + x_pair * dskip_ref[:, p * LANES:(p + 1) * LANES]
            y_ref[:, p * LANES:(p + 1) * LANES] = y

    yz = y_ref[...] * _silu(z_ref[...].astype(F32))
    ms = jnp.mean(yz * yz, axis=-1, keepdims=True)
    o_ref[...] = (yz * lax.rsqrt(ms + NORM_EPS) * nw_ref[...]).astype(o_ref.dtype)


def ssd(proj, z_col0, xbc_col0, dt_raw, conv_w, conv_b, dt_bias, a_log, d_skip, norm_w, width, heads):
    s = proj.shape[0]
    lb = SSD_BLOCK
    conv_dim = width + 2 * SSD_GROUPS * SSD_STATE
    row_vec = lambda n: pl.BlockSpec((1, n), lambda r: (0, 0))
    kernel = functools.partial(_ssd_kernel, width=width, heads=heads)
    return pl.pallas_call(
        kernel,
        grid=(s // lb,),
        in_specs=[pl.BlockSpec((lb, conv_dim), lambda r: (r, xbc_col0 // conv_dim)),
                  pl.BlockSpec((lb, width), lambda r: (r, z_col0 // width)),
                  pl.BlockSpec((lb, LANES), lambda r: (r, 0)),
                  pl.BlockSpec((SSD_CONV, conv_dim), lambda r: (0, 0)),
                  row_vec(conv_dim), row_vec(LANES), row_vec(LANES), row_vec(width), row_vec(width)],
        out_specs=pl.BlockSpec((lb, width), lambda r: (r, 0)),
        out_shape=jax.ShapeDtypeStruct((s, width), BF16),
        scratch_shapes=[pltpu.VMEM((lb + 8, conv_dim), F32),
                        pltpu.VMEM((heads // 2, SSD_STATE, LANES), F32),
                        pltpu.VMEM((lb, width), F32)],
        compiler_params=_params(("arbitrary",)),
        name="ssd",
    )(proj, proj, dt_raw, conv_w, conv_b, dt_bias, a_log, d_skip, norm_w)


def _pad_lanes(v):
    return jnp.pad(v.astype(F32), (0, LANES - v.shape[0])).reshape(1, LANES)


def kernel(x, positions, norm_mix_pre, norm_mix_post, norm_ffn_pre, norm_ffn_post, w_in, b_gate, ret_gn_w, ssd_conv_w, ssd_conv_b, ssd_dt_bias, ssd_a_log, ssd_d, ssd_norm_w, w_branch_ret, w_branch_sb, w_branch_ssd, w_out, ffn_w_gate, ffn_w_up, ffn_w_down):
    b, s, d = x.shape
    assert b == 1
    depth = w_in.shape[0]
    width = w_branch_ret.shape[1]
    ret_heads = width // HEAD_DIM
    sb_heads = width // HEAD_DIM
    ssd_heads = ssd_d.shape[1]
    conv_dim = ssd_conv_w.shape[2]
    dt_col = 4 * width + 3 * width + width + conv_dim
    ret_col, sb_col = 0, 4 * width
    z_col, xbc_col = 7 * width, 8 * width
    assert dt_col % LANES == 0 and ssd_heads <= LANES

    n_in = w_in.shape[2]
    w_in_t = jnp.swapaxes(w_in, 1, 2).reshape(depth * n_in, d)
    w_ret, w_sb, w_ssd = (w.astype(BF16) for w in (w_branch_ret, w_branch_sb, w_branch_ssd))
    w_out_b = w_out.astype(BF16)
    w_down_b = ffn_w_down.astype(BF16)

    xf = x.reshape(s, d)
    cos, sin = rotary_tables(positions.reshape(s, 1), 1024)

    h = rms_norm(xf, norm_mix_pre[0].reshape(1, d), 1024)
    for l in range(depth):
        proj = matmul_nt(h, w_in_t, l * n_in, dt_col, BF16, 2048, 512)
        dt_raw = matmul_nt(h, w_in_t, l * n_in + dt_col, LANES, F32, 2048, LANES)
        gate_logits = matmul_nt(h, w_in_t, l * n_in + dt_col + ssd_heads, 3 * d, BF16, 2048, 512)

        y_ret = retention(proj, ret_col, cos, sin, ret_gn_w[l].reshape(1, width), ret_heads)
        y_sb = stickbreak(proj, sb_col, sb_heads)
        y_ssd = ssd(proj, z_col, xbc_col, dt_raw, ssd_conv_w[l], ssd_conv_b[l].reshape(1, conv_dim),
                    _pad_lanes(ssd_dt_bias[l]), _pad_lanes(ssd_a_log[l]),
                    jnp.repeat(ssd_d[l], SSD_HEAD_DIM).reshape(1, width),
                    ssd_norm_w[l].reshape(1, width), width, ssd_heads)

        merged = gated_merge(y_ret, y_sb, y_ssd, w_ret, w_sb, w_ssd, l, gate_logits,
                             b_gate[l].reshape(1, 3 * d), 1024, 512)
        xf, h = matmul_norm_res(merged, w_out_b, l, xf, norm_mix_post[l].reshape(1, d),
                                norm_ffn_pre[l].reshape(1, d), 512, 512)

        act = swiglu(h, ffn_w_gate, ffn_w_up, l, 2048, 512)
        xf, h = matmul_norm_res(act, w_down_b, l, xf, norm_ffn_post[l].reshape(1, d),
                                norm_mix_pre[(l + 1) % depth].reshape(1, d), 512, 512)

    return xf.reshape(b, s, d)
```

```python
import functools
import math

import jax
import jax.numpy as jnp
from jax import lax
from jax.experimental import pallas as pl
from jax.experimental.pallas import tpu as pltpu

F32 = jnp.float32
BF16 = jnp.bfloat16

NORM_EPS = 1e-6
ROPE_BASE = 10000.0
LOG2E = math.log2(math.e)
LANES = 128
SUBLANES = 8
HEAD_DIM = 128
RET_CHUNK = 64
RET_BLOCK = 256
SB_QUERIES = 1024
SB_KEYS = 256
SB_DEAD = -104.0
SSD_BLOCK = 128
SSD_HEAD_DIM = 64
SSD_GROUPS = 4
SSD_STATE = 128
SSD_CONV = 4
VMEM_LIMIT = 56 * 1024 * 1024


def _params(sem):
    return pltpu.CompilerParams(dimension_semantics=sem, vmem_limit_bytes=VMEM_LIMIT)


def _sigmoid(x):
    return 1.0 / (1.0 + jnp.exp(-x))


def _silu(x):
    return x * _sigmoid(x)


def _log1p(e):
    u = 1.0 + e
    d = u - 1.0
    return jnp.where(d == 0.0, e, jnp.log(u) * (e / jnp.where(d == 0.0, 1.0, d)))


def _softplus(x):
    return jnp.maximum(x, 0.0) + _log1p(jnp.exp(-jnp.abs(x)))


def _split3(x):
    hi = x.astype(BF16)
    r1 = x - hi.astype(F32)
    mid = r1.astype(BF16)
    lo = (r1 - mid.astype(F32)).astype(BF16)
    return hi, mid, lo


def _dot(a, b):
    return jnp.dot(a, b, preferred_element_type=F32)


def _dot_nt(a, b):
    return lax.dot_general(a, b, (((1,), (1,)), ((), ())), preferred_element_type=F32)


def _rms_norm_kernel(x_ref, g_ref, h_ref):
    x = x_ref[...]
    ms = jnp.mean(x * x, axis=-1, keepdims=True)
    h_ref[...] = (x * lax.rsqrt(ms + NORM_EPS) * g_ref[...]).astype(h_ref.dtype)


def rms_norm(x, g, tm):
    m, d = x.shape
    return pl.pallas_call(
        _rms_norm_kernel,
        grid=(m // tm,),
        in_specs=[pl.BlockSpec((tm, d), lambda i: (i, 0)), pl.BlockSpec((1, d), lambda i: (0, 0))],
        out_specs=pl.BlockSpec((tm, d), lambda i: (i, 0)),
        out_shape=jax.ShapeDtypeStruct((m, d), BF16),
        compiler_params=_params(("parallel",)),
        name="rms_norm",
    )(x, g)


def _matmul_nt_kernel(h_ref, wt_ref, o_ref):
    o_ref[...] = _dot_nt(h_ref[...], wt_ref[...].astype(BF16)).astype(o_ref.dtype)


def matmul_nt(h, wt, row0, n, out_dtype, tm, tn):
    m, d = h.shape
    assert row0 % SUBLANES == 0 and tn % SUBLANES == 0
    return pl.pallas_call(
        _matmul_nt_kernel,
        grid=(m // tm, n // tn),
        in_specs=[pl.BlockSpec((tm, d), lambda i, j: (i, 0)),
                  pl.BlockSpec((pl.Element(tn), pl.Element(d)),
                               lambda i, j: (pl.multiple_of(row0 + j * tn, SUBLANES), 0))],
        out_specs=pl.BlockSpec((tm, tn), lambda i, j: (i, j)),
        out_shape=jax.ShapeDtypeStruct((m, n), out_dtype),
        compiler_params=_params(("parallel", "arbitrary")),
        name="matmul_nt",
    )(h, wt)


def _swiglu_kernel(h_ref, wg_ref, wu_ref, o_ref):
    h = h_ref[...]
    gate = _dot(h, wg_ref[...].astype(BF16))
    o_ref[...] = (_silu(gate) * _dot(h, wu_ref[...].astype(BF16))).astype(o_ref.dtype)


def swiglu(h, wg_stack, wu_stack, layer, tm, tn):
    m, d = h.shape
    n = wg_stack.shape[2]
    w_spec = pl.BlockSpec((None, d, tn), lambda i, j: (layer, 0, j))
    return pl.pallas_call(
        _swiglu_kernel,
        grid=(m // tm, n // tn),
        in_specs=[pl.BlockSpec((tm, d), lambda i, j: (i, 0)), w_spec, w_spec],
        out_specs=pl.BlockSpec((tm, tn), lambda i, j: (i, j)),
        out_shape=jax.ShapeDtypeStruct((m, n), BF16),
        compiler_params=_params(("parallel", "arbitrary")),
        name="swiglu",
    )(h, wg_stack, wu_stack)


def _matmul_norm_res_kernel(a_ref, w_ref, x_ref, g_ref, gn_ref, o_ref, h_ref, f_ref):
    j = pl.program_id(1)
    nj, _, tn = f_ref.shape
    f_ref[j] = _dot(a_ref[...], w_ref[...])

    @pl.when(j == nj - 1)
    def _():
        ss = jnp.zeros((f_ref.shape[1], 1), F32)
        for c in range(nj):
            f = f_ref[c]
            ss = ss + jnp.sum(f * f, axis=-1, keepdims=True)
        inv = lax.rsqrt(ss / (nj * tn) + NORM_EPS)
        ss = jnp.zeros((f_ref.shape[1], 1), F32)
        for c in range(nj):
            cols = slice(c * tn, (c + 1) * tn)
            y = x_ref[:, cols] + f_ref[c] * inv * g_ref[:, cols]
            o_ref[:, cols] = y
            ss = ss + jnp.sum(y * y, axis=-1, keepdims=True)
        inv = lax.rsqrt(ss / (nj * tn) + NORM_EPS)
        for c in range(nj):
            cols = slice(c * tn, (c + 1) * tn)
            h_ref[:, cols] = (o_ref[:, cols] * inv * gn_ref[:, cols]).astype(h_ref.dtype)


def matmul_norm_res(a, w_stack, layer, x, g, g_next, tm, tn):
    m, kdim = a.shape
    d = w_stack.shape[2]
    row_spec = pl.BlockSpec((tm, d), lambda i, j: (i, 0))
    vec_spec = pl.BlockSpec((1, d), lambda i, j: (0, 0))
    return pl.pallas_call(
        _matmul_norm_res_kernel,
        grid=(m // tm, d // tn),
        in_specs=[pl.BlockSpec((tm, kdim), lambda i, j: (i, 0)),
                  pl.BlockSpec((None, kdim, tn), lambda i, j: (layer, 0, j)),
                  row_spec, vec_spec, vec_spec],
        out_specs=[row_spec, row_spec],
        out_shape=[jax.ShapeDtypeStruct((m, d), F32), jax.ShapeDtypeStruct((m, d), BF16)],
        scratch_shapes=[pltpu.VMEM((d // tn, tm, tn), F32)],
        compiler_params=_params(("parallel", "arbitrary")),
        name="matmul_norm_res",
    )(a, w_stack, x, g, g_next)


def _gated_merge_kernel(yr_ref, ys_ref, yd_ref, wr_ref, ws_ref, wd_ref,
                        gr_ref, gs_ref, gd_ref, br_ref, bs_ref, bd_ref, o_ref):
    def branch(y_ref, w_ref, gl_ref, b_ref):
        gate = _sigmoid(gl_ref[...].astype(F32) + b_ref[...])
        return gate * _dot(y_ref[...], w_ref[...])

    o_ref[...] = (branch(yr_ref, wr_ref, gr_ref, br_ref)
                  + branch(ys_ref, ws_ref, gs_ref, bs_ref)
                  + branch(yd_ref, wd_ref, gd_ref, bd_ref)).astype(o_ref.dtype)


def gated_merge(y_ret, y_sb, y_ssd, w_ret, w_sb, w_ssd, layer, gate_logits, b_gate, tm, tn):
    m, width = y_ret.shape
    d = w_ret.shape[2]
    nb = d // tn
    y_spec = pl.BlockSpec((tm, width), lambda i, j: (i, 0))
    w_spec = pl.BlockSpec((None, width, tn), lambda i, j: (layer, 0, j))

    def gate_spec(b):
        return pl.BlockSpec((tm, tn), lambda i, j: (i, b * nb + j))

    def bias_spec(b):
        return pl.BlockSpec((1, tn), lambda i, j: (0, b * nb + j))

    return pl.pallas_call(
        _gated_merge_kernel,
        grid=(m // tm, nb),
        in_specs=[y_spec, y_spec, y_spec, w_spec, w_spec, w_spec,
                  gate_spec(0), gate_spec(1), gate_spec(2),
                  bias_spec(0), bias_spec(1), bias_spec(2)],
        out_specs=pl.BlockSpec((tm, tn), lambda i, j: (i, j)),
        out_shape=jax.ShapeDtypeStruct((m, d), BF16),
        compiler_params=_params(("parallel", "arbitrary")),
        name="gated_merge",
    )(y_ret, y_sb, y_ssd, w_ret, w_sb, w_ssd, gate_logits, gate_logits, gate_logits, b_gate, b_gate, b_gate)


def _rotary_kernel(pos_ref, freq_ref, cos_ref, sin_ref):
    ang = pos_ref[...].astype(F32) * freq_ref[...]
    lane = lax.broadcasted_iota(jnp.int32, ang.shape, 1)
    cos_ref[...] = jnp.cos(ang)
    sin_ref[...] = jnp.where(lane < HEAD_DIM // 2, -jnp.sin(ang), jnp.sin(ang))


def rotary_tables(positions_col, ts):
    s = positions_col.shape[0]
    half = HEAD_DIM // 2
    inv_freq = ROPE_BASE ** (-2.0 * jnp.arange(half, dtype=F32) / HEAD_DIM)
    freq = jnp.concatenate([inv_freq, inv_freq]).reshape(1, HEAD_DIM)
    return pl.pallas_call(
        _rotary_kernel,
        grid=(s // ts,),
        in_specs=[pl.BlockSpec((ts, 1), lambda i: (i, 0)),
                  pl.BlockSpec((1, HEAD_DIM), lambda i: (0, 0))],
        out_specs=[pl.BlockSpec((ts, HEAD_DIM), lambda i: (i, 0)),
                   pl.BlockSpec((ts, HEAD_DIM), lambda i: (i, 0))],
        out_shape=[jax.ShapeDtypeStruct((s, HEAD_DIM), F32),
                   jax.ShapeDtypeStruct((s, HEAD_DIM), F32)],
        compiler_params=_params(("parallel",)),
        name="rotary_tables",
    )(positions_col, freq)


def _retention_kernel(q_ref, k_ref, v_ref, g_ref, cos_ref, sin_ref, lg_ref, gn_ref, o_ref,
                      state_ref, decay_ref, qd_ref, kd_ref, *, heads):
    rb = RET_BLOCK

    @pl.when(pl.program_id(0) == 0)
    def _():
        state_ref[...] = jnp.zeros_like(state_ref)
        t = lax.broadcasted_iota(jnp.int32, (rb, rb), 0)
        s = lax.broadcasted_iota(jnp.int32, (rb, rb), 1)
        dist = jnp.abs(t - s).astype(F32)
        visible = (s // RET_CHUNK) <= (t // RET_CHUNK)
        row = lax.broadcasted_iota(jnp.int32, (rb, HEAD_DIM), 0).astype(F32)
        for h in range(heads):
            lg = lg_ref[h]
            decay_ref[h] = jnp.where(visible, jnp.exp(lg[:, :1] * dist), 0.0)
            qd_ref[h] = jnp.exp(lg * (row + 1.0))
            kd_ref[h] = jnp.exp(lg * (rb - 1.0 - row))

    cos = cos_ref[...]
    sin = sin_ref[...]

    def rotate(t):
        return t * cos + pltpu.roll(t, HEAD_DIM // 2, 1) * sin

    for h in range(heads):
        lanes = slice(h * HEAD_DIM, (h + 1) * HEAD_DIM)
        q = rotate(q_ref[:, lanes].astype(F32))
        k = rotate(k_ref[:, lanes].astype(F32)) * (HEAD_DIM ** -0.5)
        v = v_ref[:, lanes]
        scores = _dot_nt(q.astype(BF16), k.astype(BF16)) * decay_ref[h]
        o = _dot(scores.astype(BF16), v)
        state = state_ref[h]
        o = o + _dot((q * qd_ref[h]).astype(BF16), state.astype(BF16))
        kdt = jnp.transpose(k * kd_ref[h]).astype(BF16)
        block_decay = jnp.exp(lg_ref[h] * float(rb))
        state_ref[h] = state * block_decay + _dot(kdt, v)

        mu = jnp.mean(o, axis=-1, keepdims=True)
        oc = o - mu
        var = jnp.mean(oc * oc, axis=-1, keepdims=True)
        o = oc * lax.rsqrt(var + NORM_EPS) * gn_ref[:, lanes]
        o_ref[:, lanes] = (o * _silu(g_ref[:, lanes].astype(F32))).astype(o_ref.dtype)


def retention(proj, col0, cos, sin, gn_w, heads):
    s = proj.shape[0]
    rb = RET_BLOCK
    width = heads * HEAD_DIM
    c0 = col0 // width
    log_gamma = jnp.log1p(-jnp.exp2(-5.0 - jnp.arange(heads, dtype=F32)))
    lg = jnp.broadcast_to(log_gamma[:, None, None], (heads, 1, HEAD_DIM))

    def col_spec(part):
        return pl.BlockSpec((rb, width), lambda r: (r, c0 + part))

    tab_spec = pl.BlockSpec((rb, HEAD_DIM), lambda r: (r, 0))
    return pl.pallas_call(
        functools.partial(_retention_kernel, heads=heads),
        grid=(s // rb,),
        in_specs=[col_spec(0), col_spec(1), col_spec(2), col_spec(3), tab_spec, tab_spec,
                  pl.BlockSpec((heads, 1, HEAD_DIM), lambda r: (0, 0, 0)),
                  pl.BlockSpec((1, width), lambda r: (0, 0))],
        out_specs=pl.BlockSpec((rb, width), lambda r: (r, 0)),
        out_shape=jax.ShapeDtypeStruct((s, width), BF16),
        scratch_shapes=[pltpu.VMEM((heads, HEAD_DIM, HEAD_DIM), F32),
                        pltpu.VMEM((heads, rb, rb), F32),
                        pltpu.VMEM((heads, rb, HEAD_DIM), F32),
                        pltpu.VMEM((heads, rb, HEAD_DIM), F32)],
        compiler_params=_params(("arbitrary",)),
        name="retention",
    )(proj, proj, proj, proj, cos, sin, lg, gn_w)


def _stickbreak_kernel(q_ref, k_ref, v_ref, o_ref, acc_ref, carry_ref, *, tq):
    tk = SB_KEYS
    hd = HEAD_DIM
    i = pl.program_id(1)
    scale = hd ** -0.5
    krow = lax.broadcasted_iota(jnp.int32, (tk, tk), 0)
    kcol = lax.broadcasted_iota(jnp.int32, (tk, tk), 1)
    neg_suffix = jnp.where(krow >= kcol, -1.0, 0.0).astype(BF16)
    neg_suffix2 = jnp.concatenate([neg_suffix, neg_suffix], axis=0)
    zeros = jnp.zeros((tk, hd), BF16)

    def block_diag(x):
        return jnp.concatenate([jnp.concatenate([x[:, :hd], zeros], axis=1),
                                jnp.concatenate([zeros, x[:, hd:]], axis=1)], axis=0)

    acc_ref[...] = jnp.zeros_like(acc_ref)
    carry_ref[...] = jnp.zeros_like(carry_ref)

    def tile(key_block, row0, rows, masked):
        keys = pl.ds(pl.multiple_of(key_block * tk, tk), tk)
        z = _dot_nt(q_ref[row0:row0 + rows, :], block_diag(k_ref[keys, :])) * scale
        sp = jnp.maximum(z, 0.0) + jnp.log(1.0 + jnp.exp2(jnp.abs(z) * -LOG2E))
        if masked:
            key = lax.broadcasted_iota(jnp.int32, (rows, 2 * tk), 1) & (tk - 1)
            mask = key < lax.broadcasted_iota(jnp.int32, (rows, 2 * tk), 0)
            sp = jnp.where(mask, sp, 0.0)
        hi = sp.astype(BF16)
        lo = (sp - hi.astype(F32)).astype(BF16)
        suffix = [_dot(jnp.concatenate([hi[:, h * tk:(h + 1) * tk], lo[:, h * tk:(h + 1) * tk]], axis=1),
                       neg_suffix2) for h in range(2)]
        carry = carry_ref[row0:row0 + rows, :]
        c0, c1 = carry[:, :hd], carry[:, hd:]
        reps = tk // LANES
        w = jnp.exp(z + jnp.concatenate(suffix, axis=1) + jnp.concatenate([c0] * reps + [c1] * reps, axis=1))
        if masked:
            w = jnp.where(mask, w, 0.0)
        acc_ref[row0:row0 + rows, :] += _dot(w.astype(BF16), block_diag(v_ref[keys, :]))
        totals = [jnp.broadcast_to(sfx[:, :1], (rows, hd)) for sfx in suffix]
        carry_ref[row0:row0 + rows, :] = carry + jnp.concatenate(totals, axis=1)

    def alive(row0, rows):
        return jnp.max(carry_ref[row0:row0 + rows, :]) > SB_DEAD

    bands = tq // tk
    for b in range(bands - 1, -1, -1):
        row0 = b * tk
        near = min(2 * tk, tq - row0)
        tile(i * bands + b, row0, near, True)
        far0 = row0 + near
        if far0 < tq:
            @pl.when(alive(far0, tq - far0))
            def _():
                tile(i * bands + b, far0, tq - far0, False)

    n_full = i * bands

    def body(n):
        tile(n_full - 1 - n, 0, tk, False)

        @pl.when(alive(tk, tq - tk))
        def _():
            tile(n_full - 1 - n, tk, tq - tk, False)

        return n + 1

    lax.while_loop(lambda n: jnp.logical_and(n < n_full, alive(0, tq)), body, 0)
    o_ref[...] = acc_ref[...].astype(o_ref.dtype)


def stickbreak(proj, col0, heads):
    s = proj.shape[0]
    tq = min(SB_QUERIES, s)
    wide = 2 * HEAD_DIM
    c0 = col0 // wide
    pairs = heads // 2
    return pl.pallas_call(
        functools.partial(_stickbreak_kernel, tq=tq),
        grid=(pairs, s // tq),
        in_specs=[pl.BlockSpec((tq, wide), lambda g, i: (i, c0 + g)),
                  pl.BlockSpec((s, wide), lambda g, i: (0, c0 + pairs + g)),
                  pl.BlockSpec((s, wide), lambda g, i: (0, c0 + 2 * pairs + g))],
        out_specs=pl.BlockSpec((tq, wide), lambda g, i: (i, g)),
        out_shape=jax.ShapeDtypeStruct((s, heads * HEAD_DIM), BF16),
        scratch_shapes=[pltpu.VMEM((tq, wide), F32), pltpu.VMEM((tq, wide), F32)],
        compiler_params=_params(("parallel", "arbitrary")),
        name="stickbreak",
    )(proj, proj, proj)


def _ssd_kernel(xbc_ref, z_ref, dt_ref, cw_ref, cb_ref, dtb_ref, alog_ref, dskip_ref, nw_ref, o_ref,
                win_ref, state_ref, y_ref, *, width, heads):
    lb = SSD_BLOCK
    pad = 8
    heads_per_group = heads // SSD_GROUPS

    @pl.when(pl.program_id(0) == 0)
    def _():
        state_ref[...] = jnp.zeros_like(state_ref)
        win_ref[0:pad, :] = jnp.zeros((pad, win_ref.shape[1]), F32)

    win_ref[pad:pad + lb, :] = xbc_ref[...].astype(F32)
    conv = cb_ref[...]
    for tap in range(SSD_CONV):
        off = pad - (SSD_CONV - 1) + tap
        conv = conv + cw_ref[tap:tap + 1, :] * win_ref[off:off + lb, :]
    win_ref[0:pad, :] = win_ref[lb:lb + pad, :]
    act = _silu(conv)
    xs = act[:, :width]
    gn = SSD_GROUPS * SSD_STATE
    bm = act[:, width:width + gn]
    cm = act[:, width + gn:width + 2 * gn]

    dt = _softplus(dt_ref[...] + dtb_ref[...])
    a = dt * (-jnp.exp(alog_ref[...]))
    row = lax.broadcasted_iota(jnp.int32, (lb, lb), 0)
    col = lax.broadcasted_iota(jnp.int32, (lb, lb), 1)
    causal = col <= row
    tri = jnp.where(causal, 1.0, 0.0).astype(BF16)
    a_hi, a_mid, a_lo = _split3(a)
    acum = _dot(tri, a_hi) + _dot(tri, a_mid) + _dot(tri, a_lo)
    acum_t = jnp.transpose(acum)
    lane = lax.broadcasted_iota(jnp.int32, (lb, LANES), 1)
    low_half = lane < SSD_HEAD_DIM

    for g in range(SSD_GROUPS):
        bg = bm[:, g * SSD_STATE:(g + 1) * SSD_STATE]
        cg = cm[:, g * SSD_STATE:(g + 1) * SSD_STATE].astype(BF16)
        cb = jnp.where(causal, _dot_nt(cg, bg.astype(BF16)), 0.0)
        bg_t = jnp.transpose(bg).astype(BF16)
        for pp in range(heads_per_group // 2):
            p = g * (heads_per_group // 2) + pp
            e0, e1 = 2 * p, 2 * p + 1
            m_parts, acol_parts, dt_parts = [], [], []
            for e in (e0, e1):
                acol = jnp.broadcast_to(acum[:, e:e + 1], (lb, lb))
                seg = acol - acum_t[e:e + 1, :]
                decay = jnp.exp(jnp.where(causal, seg, 0.0))
                m_parts.append((cb * decay).astype(BF16))
                acol_parts.append(acol)
                dt_parts.append(jnp.broadcast_to(dt[:, e:e + 1], (lb, LANES)))
            acol_pair = jnp.where(low_half, acol_parts[0], acol_parts[1])
            dt_pair = jnp.where(low_half, dt_parts[0], dt_parts[1])
            x_pair = xs[:, p * LANES:(p + 1) * LANES]
            xdt = x_pair * dt_pair
            xdt_lo = jnp.where(low_half, xdt, 0.0).astype(BF16)
            xdt_hi = jnp.where(low_half, 0.0, xdt).astype(BF16)
            y = _dot(jnp.concatenate(m_parts, axis=1), jnp.concatenate([xdt_lo, xdt_hi], axis=0))
            state = state_ref[p]
            y = y + _dot(cg, state.astype(BF16)) * jnp.exp(acol_pair)
            last = acol_pair[lb - 1:lb, :]
            to_end = jnp.exp(last - acol_pair)
            state_ref[p] = state * jnp.exp(last) + _dot(bg_t, (xdt * to_end).astype(BF16))
            y = y + x_pair * dskip_ref[:, p * LANES:(p + 1) * LANES]
            y_ref[:, p * LANES:(p + 1) * LANES] = y

    yz = y_ref[...] * _silu(z_ref[...].astype(F32))
    ms = jnp.mean(yz * yz, axis=-1, keepdims=True)
    o_ref[...] = (yz * lax.rsqrt(ms + NORM_EPS) * nw_ref[...]).astype(o_ref.dtype)


def ssd(proj, z_col0, xbc_col0, dt_raw, conv_w, conv_b, dt_bias, a_log, d_skip, norm_w, width, heads):
    s = proj.shape[0]
    lb = SSD_BLOCK
    conv_dim = width + 2 * SSD_GROUPS * SSD_STATE
    row_vec = lambda n: pl.BlockSpec((1, n), lambda r: (0, 0))
    kernel = functools.partial(_ssd_kernel, width=width, heads=heads)
    return pl.pallas_call(
        kernel,
        grid=(s // lb,),
        in_specs=[pl.BlockSpec((lb, conv_dim), lambda r: (r, xbc_col0 // conv_dim)),
                  pl.BlockSpec((lb, width), lambda r: (r, z_col0 // width)),
                  pl.BlockSpec((lb, LANES), lambda r: (r, 0)),
                  pl.BlockSpec((SSD_CONV, conv_dim), lambda r: (0, 0)),
                  row_vec(conv_dim), row_vec(LANES), row_vec(LANES), row_vec(width), row_vec(width)],
        out_specs=pl.BlockSpec((lb, width), lambda r: (r, 0)),
        out_shape=jax.ShapeDtypeStruct((s, width), BF16),
        scratch_shapes=[pltpu.VMEM((lb + 8, conv_dim), F32),
                        pltpu.VMEM((heads // 2, SSD_STATE, LANES), F32),
                        pltpu.VMEM((lb, width), F32)],
        compiler_params=_params(("arbitrary",)),
        name="ssd",
    )(proj, proj, dt_raw, conv_w, conv_b, dt_bias, a_log, d_skip, norm_w)


def _pad_lanes(v):
    return jnp.pad(v.astype(F32), (0, LANES - v.shape[0])).reshape(1, LANES)


def kernel(x, positions, norm_mix_pre, norm_mix_post, norm_ffn_pre, norm_ffn_post, w_in, b_gate, ret_gn_w, ssd_conv_w, ssd_conv_b, ssd_dt_bias, ssd_a_log, ssd_d, ssd_norm_w, w_branch_ret, w_branch_sb, w_branch_ssd, w_out, ffn_w_gate, ffn_w_up, ffn_w_down):
    b, s, d = x.shape
    assert b == 1
    depth = w_in.shape[0]
    width = w_branch_ret.shape[1]
    ret_heads = width // HEAD_DIM
    sb_heads = width // HEAD_DIM
    ssd_heads = ssd_d.shape[1]
    conv_dim = ssd_conv_w.shape[2]
    dt_col = 4 * width + 3 * width + width + conv_dim
    ret_col, sb_col = 0, 4 * width
    z_col, xbc_col = 7 * width, 8 * width
    assert dt_col % LANES == 0 and ssd_heads <= LANES

    n_in = w_in.shape[2]
    w_in_t = jnp.swapaxes(w_in, 1, 2).reshape(depth * n_in, d)
    w_ret, w_sb, w_ssd = (w.astype(BF16) for w in (w_branch_ret, w_branch_sb, w_branch_ssd))
    w_out_b = w_out.astype(BF16)
    w_down_b = ffn_w_down.astype(BF16)

    xf = x.reshape(s, d)
    cos, sin = rotary_tables(positions.reshape(s, 1), 1024)

    h = rms_norm(xf, norm_mix_pre[0].reshape(1, d), 1024)
    for l in range(depth):
        proj = matmul_nt(h, w_in_t, l * n_in, dt_col, BF16, 2048, 512)
        dt_raw = matmul_nt(h, w_in_t, l * n_in + dt_col, LANES, F32, 2048, LANES)
        gate_logits = matmul_nt(h, w_in_t, l * n_in + dt_col + ssd_heads, 3 * d, BF16, 2048, 512)

        y_ret = retention(proj, ret_col, cos, sin, ret_gn_w[l].reshape(1, width), ret_heads)
        y_sb = stickbreak(proj, sb_col, sb_heads)
        y_ssd = ssd(proj, z_col, xbc_col, dt_raw, ssd_conv_w[l], ssd_conv_b[l].reshape(1, conv_dim),
                    _pad_lanes(ssd_dt_bias[l]), _pad_lanes(ssd_a_log[l]),
                    jnp.repeat(ssd_d[l], SSD_HEAD_DIM).reshape(1, width),
                    ssd_norm_w[l].reshape(1, width), width, ssd_heads)

        merged = gated_merge(y_ret, y_sb, y_ssd, w_ret, w_sb, w_ssd, l, gate_logits,
                             b_gate[l].reshape(1, 3 * d), 1024, 512)
        xf, h = matmul_norm_res(merged, w_out_b, l, xf, norm_mix_post[l].reshape(1, d),
                                norm_ffn_pre[l].reshape(1, d), 512, 512)

        act = swiglu(h, ffn_w_gate, ffn_w_up, l, 2048, 512)
        xf, h = matmul_norm_res(act, w_down_b, l, xf, norm_ffn_post[l].reshape(1, d),
                                norm_mix_pre[(l + 1) % depth].reshape(1, d), 512, 512)

    return xf.reshape(b, s, d)
```

```python
import functools
import math

import jax
import jax.numpy as jnp
from jax import lax
from jax.experimental import pallas as pl
from jax.experimental.pallas import tpu as pltpu

F32 = jnp.float32
BF16 = jnp.bfloat16

NORM_EPS = 1e-6
ROPE_BASE = 10000.0
LOG2E = math.log2(math.e)
LANES = 128
SUBLANES = 8
HEAD_DIM = 128
RET_CHUNK = 64
RET_BLOCK = 256
SB_QUERIES = 1024
SB_KEYS = 256
SB_DEAD = -104.0
SSD_BLOCK = 128
SSD_HEAD_DIM = 64
SSD_GROUPS = 4
SSD_STATE = 128
SSD_CONV = 4
VMEM_LIMIT = 56 * 1024 * 1024


def _params(sem):
    return pltpu.CompilerParams(dimension_semantics=sem, vmem_limit_bytes=VMEM_LIMIT)


def _sigmoid(x):
    return 1.0 / (1.0 + jnp.exp(-x))


def _silu(x):
    return x * _sigmoid(x)


def _log1p(e):
    u = 1.0 + e
    d = u - 1.0
    return jnp.where(d == 0.0, e, jnp.log(u) * (e / jnp.where(d == 0.0, 1.0, d)))


def _softplus(x):
    return jnp.maximum(x, 0.0) + _log1p(jnp.exp(-jnp.abs(x)))


def _split3(x):
    hi = x.astype(BF16)
    r1 = x - hi.astype(F32)
    mid = r1.astype(BF16)
    lo = (r1 - mid.astype(F32)).astype(BF16)
    return hi, mid, lo


def _dot(a, b):
    return jnp.dot(a, b, preferred_element_type=F32)


def _dot_nt(a, b):
    return lax.dot_general(a, b, (((1,), (1,)), ((), ())), preferred_element_type=F32)


def _rms_norm_kernel(x_ref, g_ref, h_ref):
    x = x_ref[...]
    ms = jnp.mean(x * x, axis=-1, keepdims=True)
    h_ref[...] = (x * lax.rsqrt(ms + NORM_EPS) * g_ref[...]).astype(h_ref.dtype)


def rms_norm(x, g, tm):
    m, d = x.shape
    return pl.pallas_call(
        _rms_norm_kernel,
        grid=(m // tm,),
        in_specs=[pl.BlockSpec((tm, d), lambda i: (i, 0)), pl.BlockSpec((1, d), lambda i: (0, 0))],
        out_specs=pl.BlockSpec((tm, d), lambda i: (i, 0)),
        out_shape=jax.ShapeDtypeStruct((m, d), BF16),
        compiler_params=_params(("parallel",)),
        name="rms_norm",
    )(x, g)


def _matmul_nt_kernel(h_ref, wt_ref, o_ref):
    o_ref[...] = _dot_nt(h_ref[...], wt_ref[...].astype(BF16)).astype(o_ref.dtype)


def matmul_nt(h, wt, row0, n, out_dtype, tm, tn):
    m, d = h.shape
    assert row0 % SUBLANES == 0 and tn % SUBLANES == 0
    return pl.pallas_call(
        _matmul_nt_kernel,
        grid=(m // tm, n // tn),
        in_specs=[pl.BlockSpec((tm, d), lambda i, j: (i, 0)),
                  pl.BlockSpec((pl.Element(tn), pl.Element(d)),
                               lambda i, j: (pl.multiple_of(row0 + j * tn, SUBLANES), 0))],
        out_specs=pl.BlockSpec((tm, tn), lambda i, j: (i, j)),
        out_shape=jax.ShapeDtypeStruct((m, n), out_dtype),
        compiler_params=_params(("parallel", "arbitrary")),
        name="matmul_nt",
    )(h, wt)


def _swiglu_kernel(h_ref, wg_ref, wu_ref, o_ref):
    h = h_ref[...]
    gate = _dot(h, wg_ref[...].astype(BF16))
    o_ref[...] = (_silu(gate) * _dot(h, wu_ref[...].astype(BF16))).astype(o_ref.dtype)


def swiglu(h, wg_stack, wu_stack, layer, tm, tn):
    m, d = h.shape
    n = wg_stack.shape[2]
    w_spec = pl.BlockSpec((None, d, tn), lambda i, j: (layer, 0, j))
    return pl.pallas_call(
        _swiglu_kernel,
        grid=(m // tm, n // tn),
        in_specs=[pl.BlockSpec((tm, d), lambda i, j: (i, 0)), w_spec, w_spec],
        out_specs=pl.BlockSpec((tm, tn), lambda i, j: (i, j)),
        out_shape=jax.ShapeDtypeStruct((m, n), BF16),
        compiler_params=_params(("parallel", "arbitrary")),
        name="swiglu",
    )(h, wg_stack, wu_stack)


def _matmul_norm_res_kernel(a_ref, w_ref, x_ref, g_ref, *rest):
    emit_next = len(rest) == 4
    gn_ref, o_ref, h_ref, f_ref = rest if emit_next else (None, rest[0], None, rest[1])
    j = pl.program_id(1)
    nj, _, tn = f_ref.shape
    f_ref[j] = _dot(a_ref[...], w_ref[...])

    @pl.when(j == nj - 1)
    def _():
        ss = jnp.zeros((f_ref.shape[1], 1), F32)
        for c in range(nj):
            f = f_ref[c]
            ss = ss + jnp.sum(f * f, axis=-1, keepdims=True)
        inv = lax.rsqrt(ss / (nj * tn) + NORM_EPS)
        ss = jnp.zeros((f_ref.shape[1], 1), F32)
        for c in range(nj):
            cols = slice(c * tn, (c + 1) * tn)
            y = x_ref[:, cols] + f_ref[c] * inv * g_ref[:, cols]
            o_ref[:, cols] = y
            ss = ss + jnp.sum(y * y, axis=-1, keepdims=True)
        if emit_next:
            inv = lax.rsqrt(ss / (nj * tn) + NORM_EPS)
            for c in range(nj):
                cols = slice(c * tn, (c + 1) * tn)
                h_ref[:, cols] = (o_ref[:, cols] * inv * gn_ref[:, cols]).astype(h_ref.dtype)


def matmul_norm_res(a, w_stack, layer, x, g, g_next, tm, tn):
    m, kdim = a.shape
    d = w_stack.shape[2]
    row_spec = pl.BlockSpec((tm, d), lambda i, j: (i, 0))
    vec_spec = pl.BlockSpec((1, d), lambda i, j: (0, 0))
    emit_next = g_next is not None
    outs = pl.pallas_call(
        _matmul_norm_res_kernel,
        grid=(m // tm, d // tn),
        in_specs=[pl.BlockSpec((tm, kdim), lambda i, j: (i, 0)),
                  pl.BlockSpec((None, kdim, tn), lambda i, j: (layer, 0, j)),
                  row_spec, vec_spec] + ([vec_spec] if emit_next else []),
        out_specs=[row_spec] + ([row_spec] if emit_next else []),
        out_shape=[jax.ShapeDtypeStruct((m, d), F32)] + ([jax.ShapeDtypeStruct((m, d), BF16)] if emit_next else []),
        scratch_shapes=[pltpu.VMEM((d // tn, tm, tn), F32)],
        compiler_params=_params(("parallel", "arbitrary")),
        name="matmul_norm_res",
    )(a, w_stack, x, g, *([g_next] if emit_next else []))
    return (outs[0], outs[1]) if emit_next else (outs[0], None)


def _gated_merge_kernel(yr_ref, ys_ref, yd_ref, wr_ref, ws_ref, wd_ref,
                        gr_ref, gs_ref, gd_ref, br_ref, bs_ref, bd_ref, o_ref):
    def branch(y_ref, w_ref, gl_ref, b_ref):
        gate = _sigmoid(gl_ref[...].astype(F32) + b_ref[...])
        return gate * _dot(y_ref[...], w_ref[...])

    o_ref[...] = (branch(yr_ref, wr_ref, gr_ref, br_ref)
                  + branch(ys_ref, ws_ref, gs_ref, bs_ref)
                  + branch(yd_ref, wd_ref, gd_ref, bd_ref)).astype(o_ref.dtype)


def gated_merge(y_ret, y_sb, y_ssd, w_ret, w_sb, w_ssd, layer, gate_logits, b_gate, tm, tn):
    m, width = y_ret.shape
    d = w_ret.shape[2]
    nb = d // tn
    y_spec = pl.BlockSpec((tm, width), lambda i, j: (i, 0))
    w_spec = pl.BlockSpec((None, width, tn), lambda i, j: (layer, 0, j))

    def gate_spec(b):
        return pl.BlockSpec((tm, tn), lambda i, j: (i, b * nb + j))

    def bias_spec(b):
        return pl.BlockSpec((1, tn), lambda i, j: (0, b * nb + j))

    return pl.pallas_call(
        _gated_merge_kernel,
        grid=(m // tm, nb),
        in_specs=[y_spec, y_spec, y_spec, w_spec, w_spec, w_spec,
                  gate_spec(0), gate_spec(1), gate_spec(2),
                  bias_spec(0), bias_spec(1), bias_spec(2)],
        out_specs=pl.BlockSpec((tm, tn), lambda i, j: (i, j)),
        out_shape=jax.ShapeDtypeStruct((m, d), BF16),
        compiler_params=_params(("parallel", "arbitrary")),
        name="gated_merge",
    )(y_ret, y_sb, y_ssd, w_ret, w_sb, w_ssd, gate_logits, gate_logits, gate_logits, b_gate, b_gate, b_gate)


def _rotary_kernel(pos_ref, freq_ref, cos_ref, sin_ref):
    ang = pos_ref[...].astype(F32) * freq_ref[...]
    lane = lax.broadcasted_iota(jnp.int32, ang.shape, 1)
    cos_ref[...] = jnp.cos(ang)
    sin_ref[...] = jnp.where(lane < HEAD_DIM // 2, -jnp.sin(ang), jnp.sin(ang))


def rotary_tables(positions_col, ts):
    s = positions_col.shape[0]
    half = HEAD_DIM // 2
    inv_freq = ROPE_BASE ** (-2.0 * jnp.arange(half, dtype=F32) / HEAD_DIM)
    freq = jnp.concatenate([inv_freq, inv_freq]).reshape(1, HEAD_DIM)
    return pl.pallas_call(
        _rotary_kernel,
        grid=(s // ts,),
        in_specs=[pl.BlockSpec((ts, 1), lambda i: (i, 0)),
                  pl.BlockSpec((1, HEAD_DIM), lambda i: (0, 0))],
        out_specs=[pl.BlockSpec((ts, HEAD_DIM), lambda i: (i, 0)),
                   pl.BlockSpec((ts, HEAD_DIM), lambda i: (i, 0))],
        out_shape=[jax.ShapeDtypeStruct((s, HEAD_DIM), F32),
                   jax.ShapeDtypeStruct((s, HEAD_DIM), F32)],
        compiler_params=_params(("parallel",)),
        name="rotary_tables",
    )(positions_col, freq)


def _retention_kernel(q_ref, k_ref, v_ref, g_ref, cos_ref, sin_ref, lg_ref, gn_ref, o_ref,
                      state_ref, decay_ref, qd_ref, kd_ref, *, heads):
    rb = RET_BLOCK

    @pl.when(pl.program_id(0) == 0)
    def _():
        state_ref[...] = jnp.zeros_like(state_ref)
        t = lax.broadcasted_iota(jnp.int32, (rb, rb), 0)
        s = lax.broadcasted_iota(jnp.int32, (rb, rb), 1)
        dist = jnp.abs(t - s).astype(F32)
        visible = (s // RET_CHUNK) <= (t // RET_CHUNK)
        row = lax.broadcasted_iota(jnp.int32, (rb, HEAD_DIM), 0).astype(F32)
        for h in range(heads):
            lg = lg_ref[h]
            decay_ref[h] = jnp.where(visible, jnp.exp(lg[:, :1] * dist), 0.0)
            qd_ref[h] = jnp.exp(lg * (row + 1.0))
            kd_ref[h] = jnp.exp(lg * (rb - 1.0 - row))

    cos = cos_ref[...]
    sin = sin_ref[...]

    def rotate(t):
        return t * cos + pltpu.roll(t, HEAD_DIM // 2, 1) * sin

    for h in range(heads):
        lanes = slice(h * HEAD_DIM, (h + 1) * HEAD_DIM)
        q = rotate(q_ref[:, lanes].astype(F32))
        k = rotate(k_ref[:, lanes].astype(F32)) * (HEAD_DIM ** -0.5)
        v = v_ref[:, lanes]
        scores = _dot_nt(q.astype(BF16), k.astype(BF16)) * decay_ref[h]
        o = _dot(scores.astype(BF16), v)
        state = state_ref[h]
        o = o + _dot((q * qd_ref[h]).astype(BF16), state.astype(BF16))
        kdt = jnp.transpose(k * kd_ref[h]).astype(BF16)
        block_decay = jnp.exp(lg_ref[h] * float(rb))
        state_ref[h] = state * block_decay + _dot(kdt, v)

        mu = jnp.mean(o, axis=-1, keepdims=True)
        oc = o - mu
        var = jnp.mean(oc * oc, axis=-1, keepdims=True)
        o = oc * lax.rsqrt(var + NORM_EPS) * gn_ref[:, lanes]
        o_ref[:, lanes] = (o * _silu(g_ref[:, lanes].astype(F32))).astype(o_ref.dtype)


def retention(proj, col0, cos, sin, gn_w, heads):
    s = proj.shape[0]
    rb = RET_BLOCK
    width = heads * HEAD_DIM
    c0 = col0 // width
    log_gamma = jnp.log1p(-jnp.exp2(-5.0 - jnp.arange(heads, dtype=F32)))
    lg = jnp.broadcast_to(log_gamma[:, None, None], (heads, 1, HEAD_DIM))

    def col_spec(part):
        return pl.BlockSpec((rb, width), lambda r: (r, c0 + part))

    tab_spec = pl.BlockSpec((rb, HEAD_DIM), lambda r: (r, 0))
    return pl.pallas_call(
        functools.partial(_retention_kernel, heads=heads),
        grid=(s // rb,),
        in_specs=[col_spec(0), col_spec(1), col_spec(2), col_spec(3), tab_spec, tab_spec,
                  pl.BlockSpec((heads, 1, HEAD_DIM), lambda r: (0, 0, 0)),
                  pl.BlockSpec((1, width), lambda r: (0, 0))],
        out_specs=pl.BlockSpec((rb, width), lambda r: (r, 0)),
        out_shape=jax.ShapeDtypeStruct((s, width), BF16),
        scratch_shapes=[pltpu.VMEM((heads, HEAD_DIM, HEAD_DIM), F32),
                        pltpu.VMEM((heads, rb, rb), F32),
                        pltpu.VMEM((heads, rb, HEAD_DIM), F32),
                        pltpu.VMEM((heads, rb, HEAD_DIM), F32)],
        compiler_params=_params(("arbitrary",)),
        name="retention",
    )(proj, proj, proj, proj, cos, sin, lg, gn_w)


def _stickbreak_kernel(q_ref, k_ref, v_ref, o_ref, acc_ref, carry_ref, *, tq):
    tk = SB_KEYS
    hd = HEAD_DIM
    i = pl.program_id(1)
    scale = hd ** -0.5
    krow = lax.broadcasted_iota(jnp.int32, (tk, tk), 0)
    kcol = lax.broadcasted_iota(jnp.int32, (tk, tk), 1)
    neg_suffix = jnp.where(krow >= kcol, -1.0, 0.0).astype(BF16)
    neg_suffix2 = jnp.concatenate([neg_suffix, neg_suffix], axis=0)
    zeros = jnp.zeros((tk, hd), BF16)

    def block_diag(x):
        return jnp.concatenate([jnp.concatenate([x[:, :hd], zeros], axis=1),
                                jnp.concatenate([zeros, x[:, hd:]], axis=1)], axis=0)

    acc_ref[...] = jnp.zeros_like(acc_ref)
    carry_ref[...] = jnp.zeros_like(carry_ref)

    def key_rows(key_block):
        return pl.ds(pl.multiple_of(key_block * tk, tk), tk)

    def causal_mask(rows):
        key = lax.broadcasted_iota(jnp.int32, (rows, 2 * tk), 1) & (tk - 1)
        return key < lax.broadcasted_iota(jnp.int32, (rows, 2 * tk), 0)

    def scores(key_block, row0, rows):
        return _dot_nt(q_ref[row0:row0 + rows, :], block_diag(k_ref[key_rows(key_block), :])) * scale

    def suffix_sums(z, rows, masked):
        sp = jnp.maximum(z, 0.0) + jnp.log(1.0 + jnp.exp2(jnp.abs(z) * -LOG2E))
        if masked:
            sp = jnp.where(causal_mask(rows), sp, 0.0)
        hi = sp.astype(BF16)
        lo = (sp - hi.astype(F32)).astype(BF16)
        suffix = [_dot(jnp.concatenate([hi[:, h * tk:(h + 1) * tk], lo[:, h * tk:(h + 1) * tk]], axis=1),
                       neg_suffix2) for h in range(2)]
        totals = jnp.concatenate([jnp.broadcast_to(sfx[:, :1], (rows, hd)) for sfx in suffix], axis=1)
        return z + jnp.concatenate(suffix, axis=1), totals

    def attend(key_block, row0, rows, masked, arg, totals, valid=None):
        carry = carry_ref[row0:row0 + rows, :]
        c0, c1 = carry[:, :hd], carry[:, hd:]
        reps = tk // LANES
        w = jnp.exp(arg + jnp.concatenate([c0] * reps + [c1] * reps, axis=1))
        if masked:
            w = jnp.where(causal_mask(rows), w, 0.0)
        if valid is not None:
            w = jnp.where(valid, w, 0.0)
            totals = jnp.where(valid, totals, 0.0)
        acc_ref[row0:row0 + rows, :] += _dot(w.astype(BF16), block_diag(v_ref[key_rows(key_block), :]))
        carry_ref[row0:row0 + rows, :] = carry + totals

    def tile(key_block, row0, rows):
        attend(key_block, row0, rows, False, *suffix_sums(scores(key_block, row0, rows), rows, False))

    def alive(row0, rows):
        return jnp.max(carry_ref[row0:row0 + rows, :]) > SB_DEAD

    bands = tq // tk
    n_full = i * bands
    near = [(i * bands + b, b * tk, min(2 * tk, tq - b * tk), True) for b in range(bands - 1, -1, -1)]
    near.append((jnp.maximum(n_full - 1, 0), 0, tk, False))
    zs = [scores(kb, row0, rows) for kb, row0, rows, _ in near]
    sums = [suffix_sums(z, rows, masked) for z, (_, _, rows, masked) in zip(zs, near)]
    for (kb, row0, rows, masked), (arg, totals) in zip(near, sums):
        attend(kb, row0, rows, masked, arg, totals, None if masked else n_full > 0)
    for b in range(bands - 1, -1, -1):
        far0 = (b + 2) * tk
        if far0 < tq:
            @pl.when(alive(far0, tq - far0))
            def _():
                tile(i * bands + b, far0, tq - far0)

    @pl.when(jnp.logical_and(n_full > 0, alive(tk, tq - tk)))
    def _():
        tile(n_full - 1, tk, tq - tk)

    def body(n):
        tile(n_full - 1 - n, 0, tq)
        return n + 1

    lax.while_loop(lambda n: jnp.logical_and(n < n_full, alive(0, tq)), body, 1)
    o_ref[...] = acc_ref[...].astype(o_ref.dtype)


def stickbreak(proj, col0, heads):
    s = proj.shape[0]
    tq = min(SB_QUERIES, s)
    wide = 2 * HEAD_DIM
    c0 = col0 // wide
    pairs = heads // 2
    return pl.pallas_call(
        functools.partial(_stickbreak_kernel, tq=tq),
        grid=(pairs, s // tq),
        in_specs=[pl.BlockSpec((tq, wide), lambda g, i: (i, c0 + g)),
                  pl.BlockSpec((s, wide), lambda g, i: (0, c0 + pairs + g)),
                  pl.BlockSpec((s, wide), lambda g, i: (0, c0 + 2 * pairs + g))],
        out_specs=pl.BlockSpec((tq, wide), lambda g, i: (i, g)),
        out_shape=jax.ShapeDtypeStruct((s, heads * HEAD_DIM), BF16),
        scratch_shapes=[pltpu.VMEM((tq, wide), F32), pltpu.VMEM((tq, wide), F32)],
        compiler_params=_params(("parallel", "arbitrary")),
        name="stickbreak",
    )(proj, proj, proj)


def _ssd_kernel(xbc_ref, z_ref, dt_ref, cw_ref, cb_ref, dtb_ref, alog_ref, dskip_ref, nw_ref, o_ref,
                win_ref, state_ref, y_ref, *, width, heads):
    lb = SSD_BLOCK
    pad = 8
    heads_per_group = heads // SSD_GROUPS

    @pl.when(pl.program_id(0) == 0)
    def _():
        state_ref[...] = jnp.zeros_like(state_ref)
        win_ref[0:pad, :] = jnp.zeros((pad, win_ref.shape[1]), F32)

    win_ref[pad:pad + lb, :] = xbc_ref[...].astype(F32)
    conv = cb_ref[...]
    for tap in range(SSD_CONV):
        off = pad - (SSD_CONV - 1) + tap
        conv = conv + cw_ref[tap:tap + 1, :] * win_ref[off:off + lb, :]
    win_ref[0:pad, :] = win_ref[lb:lb + pad, :]
    act = _silu(conv)
    xs = act[:, :width]
    gn = SSD_GROUPS * SSD_STATE
    bm = act[:, width:width + gn]
    cm = act[:, width + gn:width + 2 * gn]

    dt = _softplus(dt_ref[...] + dtb_ref[...])
    a = dt * (-jnp.exp(alog_ref[...]))
    row = lax.broadcasted_iota(jnp.int32, (lb, lb), 0)
    col = lax.broadcasted_iota(jnp.int32, (lb, lb), 1)
    causal = col <= row
    tri = jnp.where(causal, 1.0, 0.0).astype(BF16)
    a_hi, a_mid, a_lo = _split3(a)
    acum = _dot(tri, a_hi) + _dot(tri, a_mid) + _dot(tri, a_lo)
    acum_t = jnp.transpose(acum)
    lane = lax.broadcasted_iota(jnp.int32, (lb, LANES), 1)
    low_half = lane < SSD_HEAD_DIM

    for g in range(SSD_GROUPS):
        bg = bm[:, g * SSD_STATE:(g + 1) * SSD_STATE]
        cg = cm[:, g * SSD_STATE:(g + 1) * SSD_STATE].astype(BF16)
        cb = jnp.where(causal, _dot_nt(cg, bg.astype(BF16)), 0.0)
        bg_t = jnp.transpose(bg).astype(BF16)
        for pp in range(heads_per_group // 2):
            p = g * (heads_per_group // 2) + pp
            e0, e1 = 2 * p, 2 * p + 1
            m_parts, acol_parts, dt_parts = [], [], []
            for e in (e0, e1):
                acol = jnp.broadcast_to(acum[:, e:e + 1], (lb, lb))
                seg = acol - acum_t[e:e + 1, :]
                decay = jnp.exp(jnp.where(causal, seg, 0.0))
                m_parts.append((cb * decay).astype(BF16))
                acol_parts.append(acol)
                dt_parts.append(jnp.broadcast_to(dt[:, e:e + 1], (lb, LANES)))
            acol_pair = jnp.where(low_half, acol_parts[0], acol_parts[1])
            dt_pair = jnp.where(low_half, dt_parts[0], dt_parts[1])
            x_pair = xs[:, p * LANES:(p + 1) * LANES]
            xdt = x_pair * dt_pair
            xdt_lo = jnp.where(low_half, xdt, 0.0).astype(BF16)
            xdt_hi = jnp.where(low_half, 0.0, xdt).astype(BF16)
            y = _dot(jnp.concatenate(m_parts, axis=1), jnp.concatenate([xdt_lo, xdt_hi], axis=0))
            state = state_ref[p]
            y = y + _dot(cg, state.astype(BF16)) * jnp.exp(acol_pair)
            last = acol_pair[lb - 1:lb, :]
            to_end = jnp.exp(last - acol_pair)
            state_ref[p] = state * jnp.exp(last) + _dot(bg_t, (xdt * to_end).astype(BF16))
            y = y + x_pair * dskip_ref[:, p * LANES:(p + 1) * LANES]
            y_ref[:, p * LANES:(p + 1) * LANES] = y

    yz = y_ref[...] * _silu(z_ref[...].astype(F32))
    ms = jnp.mean(yz * yz, axis=-1, keepdims=True)
    o_ref[...] = (yz * lax.rsqrt(ms + NORM_EPS) * nw_ref[...]).astype(o_ref.dtype)


def ssd(proj, z_col0, xbc_col0, dt_raw, conv_w, conv_b, dt_bias, a_log, d_skip, norm_w, width, heads):
    s = proj.shape[0]
    lb = SSD_BLOCK
    conv_dim = width + 2 * SSD_GROUPS * SSD_STATE
    row_vec = lambda n: pl.BlockSpec((1, n), lambda r: (0, 0))
    kernel = functools.partial(_ssd_kernel, width=width, heads=heads)
    return pl.pallas_call(
        kernel,
        grid=(s // lb,),
        in_specs=[pl.BlockSpec((lb, conv_dim), lambda r: (r, xbc_col0 // conv_dim)),
                  pl.BlockSpec((lb, width), lambda r: (r, z_col0 // width)),
                  pl.BlockSpec((lb, LANES), lambda r: (r, 0)),
                  pl.BlockSpec((SSD_CONV, conv_dim), lambda r: (0, 0)),
                  row_vec(conv_dim), row_vec(LANES), row_vec(LANES), row_vec(width), row_vec(width)],
        out_specs=pl.BlockSpec((lb, width), lambda r: (r, 0)),
        out_shape=jax.ShapeDtypeStruct((s, width), BF16),
        scratch_shapes=[pltpu.VMEM((lb + 8, conv_dim), F32),
                        pltpu.VMEM((heads // 2, SSD_STATE, LANES), F32),
                        pltpu.VMEM((lb, width), F32)],
        compiler_params=_params(("arbitrary",)),
        name="ssd",
    )(proj, proj, dt_raw, conv_w, conv_b, dt_bias, a_log, d_skip, norm_w)


def _pad_lanes(v):
    return jnp.pad(v.astype(F32), (0, LANES - v.shape[0])).reshape(1, LANES)


def kernel(x, positions, norm_mix_pre, norm_mix_post, norm_ffn_pre, norm_ffn_post, w_in, b_gate, ret_gn_w, ssd_conv_w, ssd_conv_b, ssd_dt_bias, ssd_a_log, ssd_d, ssd_norm_w, w_branch_ret, w_branch_sb, w_branch_ssd, w_out, ffn_w_gate, ffn_w_up, ffn_w_down):
    b, s, d = x.shape
    assert b == 1
    depth = w_in.shape[0]
    width = w_branch_ret.shape[1]
    ret_heads = width // HEAD_DIM
    sb_heads = width // HEAD_DIM
    ssd_heads = ssd_d.shape[1]
    conv_dim = ssd_conv_w.shape[2]
    dt_col = 4 * width + 3 * width + width + conv_dim
    ret_col, sb_col = 0, 4 * width
    z_col, xbc_col = 7 * width, 8 * width
    assert dt_col % LANES == 0 and ssd_heads <= LANES

    n_in = w_in.shape[2]
    w_in_t = jnp.swapaxes(w_in, 1, 2).reshape(depth * n_in, d)
    w_ret, w_sb, w_ssd = (w.astype(BF16) for w in (w_branch_ret, w_branch_sb, w_branch_ssd))
    w_out_b = w_out.astype(BF16)
    w_down_b = ffn_w_down.astype(BF16)

    xf = x.reshape(s, d)
    cos, sin = rotary_tables(positions.reshape(s, 1), 1024)

    h = rms_norm(xf, norm_mix_pre[0].reshape(1, d), 1024)
    for l in range(depth):
        proj = matmul_nt(h, w_in_t, l * n_in, dt_col, BF16, 2048, 512)
        dt_raw = matmul_nt(h, w_in_t, l * n_in + dt_col, LANES, F32, 2048, LANES)
        gate_logits = matmul_nt(h, w_in_t, l * n_in + dt_col + ssd_heads, 3 * d, BF16, 2048, 512)

        y_ret = retention(proj, ret_col, cos, sin, ret_gn_w[l].reshape(1, width), ret_heads)
        y_sb = stickbreak(proj, sb_col, sb_heads)
        y_ssd = ssd(proj, z_col, xbc_col, dt_raw, ssd_conv_w[l], ssd_conv_b[l].reshape(1, conv_dim),
                    _pad_lanes(ssd_dt_bias[l]), _pad_lanes(ssd_a_log[l]),
                    jnp.repeat(ssd_d[l], SSD_HEAD_DIM).reshape(1, width),
                    ssd_norm_w[l].reshape(1, width), width, ssd_heads)

        merged = gated_merge(y_ret, y_sb, y_ssd, w_ret, w_sb, w_ssd, l, gate_logits,
                             b_gate[l].reshape(1, 3 * d), 1024, 512)
        xf, h = matmul_norm_res(merged, w_out_b, l, xf, norm_mix_post[l].reshape(1, d),
                                norm_ffn_pre[l].reshape(1, d), 512, 512)

        act = swiglu(h, ffn_w_gate, ffn_w_up, l, 2048, 512)
        g_next = norm_mix_pre[l + 1].reshape(1, d) if l + 1 < depth else None
        xf, h = matmul_norm_res(act, w_down_b, l, xf, norm_ffn_post[l].reshape(1, d), g_next, 512, 512)

    return xf.reshape(b, s, d)
```

```python
import functools
import math

import jax
import jax.numpy as jnp
from jax import lax
from jax.experimental import pallas as pl
from jax.experimental.pallas import tpu as pltpu

F32 = jnp.float32
BF16 = jnp.bfloat16

NORM_EPS = 1e-6
ROPE_BASE = 10000.0
LOG2E = math.log2(math.e)
LANES = 128
SUBLANES = 8
HEAD_DIM = 128
RET_CHUNK = 64
RET_BLOCK = 256
SB_QUERIES = 1024
SB_KEYS = 256
SB_DEAD = -104.0
SSD_BLOCK = 128
SSD_HEAD_DIM = 64
SSD_GROUPS = 4
SSD_STATE = 128
SSD_CONV = 4
VMEM_LIMIT = 56 * 1024 * 1024


VMEM_LIMIT_RESIDENT = 60 * 1024 * 1024


def _params(sem, vmem_limit=VMEM_LIMIT):
    return pltpu.CompilerParams(dimension_semantics=sem, vmem_limit_bytes=vmem_limit)


def _sigmoid(x):
    return 1.0 / (1.0 + jnp.exp(-x))


def _silu(x):
    return x * _sigmoid(x)


def _log1p(e):
    u = 1.0 + e
    d = u - 1.0
    return jnp.where(d == 0.0, e, jnp.log(u) * (e / jnp.where(d == 0.0, 1.0, d)))


def _softplus(x):
    return jnp.maximum(x, 0.0) + _log1p(jnp.exp(-jnp.abs(x)))


def _split3(x):
    hi = x.astype(BF16)
    r1 = x - hi.astype(F32)
    mid = r1.astype(BF16)
    lo = (r1 - mid.astype(F32)).astype(BF16)
    return hi, mid, lo


def _dot(a, b):
    return jnp.dot(a, b, preferred_element_type=F32)


def _dot_nt(a, b):
    return lax.dot_general(a, b, (((1,), (1,)), ((), ())), preferred_element_type=F32)


def _rms_norm_kernel(x_ref, g_ref, h_ref):
    x = x_ref[...]
    ms = jnp.mean(x * x, axis=-1, keepdims=True)
    h_ref[...] = (x * lax.rsqrt(ms + NORM_EPS) * g_ref[...]).astype(h_ref.dtype)


def rms_norm(x, g, tm):
    m, d = x.shape
    return pl.pallas_call(
        _rms_norm_kernel,
        grid=(m // tm,),
        in_specs=[pl.BlockSpec((tm, d), lambda i: (i, 0)), pl.BlockSpec((1, d), lambda i: (0, 0))],
        out_specs=pl.BlockSpec((tm, d), lambda i: (i, 0)),
        out_shape=jax.ShapeDtypeStruct((m, d), BF16),
        compiler_params=_params(("parallel",)),
        name="rms_norm",
    )(x, g)


def _matmul_nt_kernel(h_ref, wt_ref, o_ref):
    o_ref[...] = _dot_nt(h_ref[...], wt_ref[...].astype(BF16)).astype(o_ref.dtype)


def matmul_nt(h, wt, row0, n, out_dtype, tm, tn):
    m, d = h.shape
    assert row0 % SUBLANES == 0 and tn % SUBLANES == 0
    return pl.pallas_call(
        _matmul_nt_kernel,
        grid=(m // tm, n // tn),
        in_specs=[pl.BlockSpec((tm, d), lambda i, j: (i, 0)),
                  pl.BlockSpec((pl.Element(tn), pl.Element(d)),
                               lambda i, j: (pl.multiple_of(row0 + j * tn, SUBLANES), 0))],
        out_specs=pl.BlockSpec((tm, tn), lambda i, j: (i, j)),
        out_shape=jax.ShapeDtypeStruct((m, n), out_dtype),
        compiler_params=_params(("parallel", "arbitrary")),
        name="matmul_nt",
    )(h, wt)


def _swiglu_kernel(h_ref, wg_ref, wu_ref, o_ref):
    h = h_ref[...]
    gate = _dot(h, wg_ref[...].astype(BF16))
    o_ref[...] = (_silu(gate) * _dot(h, wu_ref[...].astype(BF16))).astype(o_ref.dtype)


def swiglu(h, wg_stack, wu_stack, layer, tm, tn):
    m, d = h.shape
    n = wg_stack.shape[2]
    w_spec = pl.BlockSpec((None, d, tn), lambda i, j: (layer, 0, j))
    return pl.pallas_call(
        _swiglu_kernel,
        grid=(m // tm, n // tn),
        in_specs=[pl.BlockSpec((tm, d), lambda i, j: (i, 0)), w_spec, w_spec],
        out_specs=pl.BlockSpec((tm, tn), lambda i, j: (i, j)),
        out_shape=jax.ShapeDtypeStruct((m, n), BF16),
        compiler_params=_params(("parallel", "arbitrary")),
        name="swiglu",
    )(h, wg_stack, wu_stack)


def _matmul_norm_res_kernel(a_ref, w_ref, x_ref, g_ref, *rest):
    emit_next = len(rest) == 4
    gn_ref, o_ref, h_ref, f_ref = rest if emit_next else (None, rest[0], None, rest[1])
    j = pl.program_id(1)
    nj, _, tn = f_ref.shape
    f_ref[j] = _dot(a_ref[...], w_ref[...])

    @pl.when(j == nj - 1)
    def _():
        ss = jnp.zeros((f_ref.shape[1], 1), F32)
        for c in range(nj):
            f = f_ref[c]
            ss = ss + jnp.sum(f * f, axis=-1, keepdims=True)
        inv = lax.rsqrt(ss / (nj * tn) + NORM_EPS)
        ss = jnp.zeros((f_ref.shape[1], 1), F32)
        for c in range(nj):
            cols = slice(c * tn, (c + 1) * tn)
            y = x_ref[:, cols] + f_ref[c] * inv * g_ref[:, cols]
            o_ref[:, cols] = y
            ss = ss + jnp.sum(y * y, axis=-1, keepdims=True)
        if emit_next:
            inv = lax.rsqrt(ss / (nj * tn) + NORM_EPS)
            for c in range(nj):
                cols = slice(c * tn, (c + 1) * tn)
                h_ref[:, cols] = (o_ref[:, cols] * inv * gn_ref[:, cols]).astype(h_ref.dtype)


def matmul_norm_res(a, w_stack, layer, x, g, g_next, tm, tn):
    m, kdim = a.shape
    d = w_stack.shape[2]
    row_spec = pl.BlockSpec((tm, d), lambda i, j: (i, 0))
    vec_spec = pl.BlockSpec((1, d), lambda i, j: (0, 0))
    emit_next = g_next is not None
    outs = pl.pallas_call(
        _matmul_norm_res_kernel,
        grid=(m // tm, d // tn),
        in_specs=[pl.BlockSpec((tm, kdim), lambda i, j: (i, 0)),
                  pl.BlockSpec((None, kdim, tn), lambda i, j: (layer, 0, j),
                               **({"pipeline_mode": pl.Buffered(1)} if tn == d else {})),
                  row_spec, vec_spec] + ([vec_spec] if emit_next else []),
        out_specs=[row_spec] + ([row_spec] if emit_next else []),
        out_shape=[jax.ShapeDtypeStruct((m, d), F32)] + ([jax.ShapeDtypeStruct((m, d), BF16)] if emit_next else []),
        scratch_shapes=[pltpu.VMEM((d // tn, tm, tn), F32)],
        compiler_params=_params(("parallel", "arbitrary"), VMEM_LIMIT_RESIDENT if tn == d else VMEM_LIMIT),
        name="matmul_norm_res",
    )(a, w_stack, x, g, *([g_next] if emit_next else []))
    return (outs[0], outs[1]) if emit_next else (outs[0], None)


def _gated_merge_kernel(yr_ref, ys_ref, yd_ref, wr_ref, ws_ref, wd_ref,
                        gr_ref, gs_ref, gd_ref, br_ref, bs_ref, bd_ref, o_ref):
    def branch(y_ref, w_ref, gl_ref, b_ref):
        gate = _sigmoid(gl_ref[...].astype(F32) + b_ref[...])
        return gate * _dot(y_ref[...], w_ref[...])

    o_ref[...] = (branch(yr_ref, wr_ref, gr_ref, br_ref)
                  + branch(ys_ref, ws_ref, gs_ref, bs_ref)
                  + branch(yd_ref, wd_ref, gd_ref, bd_ref)).astype(o_ref.dtype)


def gated_merge(y_ret, y_sb, y_ssd, w_ret, w_sb, w_ssd, layer, gate_logits, b_gate, tm, tn):
    m, width = y_ret.shape
    d = w_ret.shape[2]
    nb = d // tn
    y_spec = pl.BlockSpec((tm, width), lambda i, j: (i, 0))
    w_spec = pl.BlockSpec((None, width, tn), lambda i, j: (layer, 0, j))

    def gate_spec(b):
        return pl.BlockSpec((tm, tn), lambda i, j: (i, b * nb + j))

    def bias_spec(b):
        return pl.BlockSpec((1, tn), lambda i, j: (0, b * nb + j))

    return pl.pallas_call(
        _gated_merge_kernel,
        grid=(m // tm, nb),
        in_specs=[y_spec, y_spec, y_spec, w_spec, w_spec, w_spec,
                  gate_spec(0), gate_spec(1), gate_spec(2),
                  bias_spec(0), bias_spec(1), bias_spec(2)],
        out_specs=pl.BlockSpec((tm, tn), lambda i, j: (i, j)),
        out_shape=jax.ShapeDtypeStruct((m, d), BF16),
        compiler_params=_params(("parallel", "arbitrary")),
        name="gated_merge",
    )(y_ret, y_sb, y_ssd, w_ret, w_sb, w_ssd, gate_logits, gate_logits, gate_logits, b_gate, b_gate, b_gate)


def _rotary_kernel(pos_ref, freq_ref, cos_ref, sin_ref):
    ang = pos_ref[...].astype(F32) * freq_ref[...]
    lane = lax.broadcasted_iota(jnp.int32, ang.shape, 1)
    cos_ref[...] = jnp.cos(ang)
    sin_ref[...] = jnp.where(lane < HEAD_DIM // 2, -jnp.sin(ang), jnp.sin(ang))


def rotary_tables(positions_col, ts):
    s = positions_col.shape[0]
    half = HEAD_DIM // 2
    inv_freq = ROPE_BASE ** (-2.0 * jnp.arange(half, dtype=F32) / HEAD_DIM)
    freq = jnp.concatenate([inv_freq, inv_freq]).reshape(1, HEAD_DIM)
    return pl.pallas_call(
        _rotary_kernel,
        grid=(s // ts,),
        in_specs=[pl.BlockSpec((ts, 1), lambda i: (i, 0)),
                  pl.BlockSpec((1, HEAD_DIM), lambda i: (0, 0))],
        out_specs=[pl.BlockSpec((ts, HEAD_DIM), lambda i: (i, 0)),
                   pl.BlockSpec((ts, HEAD_DIM), lambda i: (i, 0))],
        out_shape=[jax.ShapeDtypeStruct((s, HEAD_DIM), F32),
                   jax.ShapeDtypeStruct((s, HEAD_DIM), F32)],
        compiler_params=_params(("parallel",)),
        name="rotary_tables",
    )(positions_col, freq)


def _retention_kernel(q_ref, k_ref, v_ref, g_ref, cos_ref, sin_ref, lg_ref, gn_ref, o_ref,
                      state_ref, decay_ref, qd_ref, kd_ref, *, heads):
    rb = RET_BLOCK

    @pl.when(pl.program_id(0) == 0)
    def _():
        state_ref[...] = jnp.zeros_like(state_ref)
        t = lax.broadcasted_iota(jnp.int32, (rb, rb), 0)
        s = lax.broadcasted_iota(jnp.int32, (rb, rb), 1)
        dist = jnp.abs(t - s).astype(F32)
        visible = (s // RET_CHUNK) <= (t // RET_CHUNK)
        row = lax.broadcasted_iota(jnp.int32, (rb, HEAD_DIM), 0).astype(F32)
        for h in range(heads):
            lg = lg_ref[h]
            decay_ref[h] = jnp.where(visible, jnp.exp(lg[:, :1] * dist), 0.0)
            qd_ref[h] = jnp.exp(lg * (row + 1.0))
            kd_ref[h] = jnp.exp(lg * (rb - 1.0 - row))

    cos = cos_ref[...]
    sin = sin_ref[...]

    def rotate(t):
        return t * cos + pltpu.roll(t, HEAD_DIM // 2, 1) * sin

    for h in range(heads):
        lanes = slice(h * HEAD_DIM, (h + 1) * HEAD_DIM)
        q = rotate(q_ref[:, lanes].astype(F32))
        k = rotate(k_ref[:, lanes].astype(F32)) * (HEAD_DIM ** -0.5)
        v = v_ref[:, lanes]
        scores = _dot_nt(q.astype(BF16), k.astype(BF16)) * decay_ref[h]
        o = _dot(scores.astype(BF16), v)
        state = state_ref[h]
        o = o + _dot((q * qd_ref[h]).astype(BF16), state.astype(BF16))
        kdt = jnp.transpose(k * kd_ref[h]).astype(BF16)
        block_decay = jnp.exp(lg_ref[h] * float(rb))
        state_ref[h] = state * block_decay + _dot(kdt, v)

        mu = jnp.mean(o, axis=-1, keepdims=True)
        oc = o - mu
        var = jnp.mean(oc * oc, axis=-1, keepdims=True)
        o = oc * lax.rsqrt(var + NORM_EPS) * gn_ref[:, lanes]
        o_ref[:, lanes] = (o * _silu(g_ref[:, lanes].astype(F32))).astype(o_ref.dtype)


def retention(proj, col0, cos, sin, gn_w, heads):
    s = proj.shape[0]
    rb = RET_BLOCK
    width = heads * HEAD_DIM
    c0 = col0 // width
    log_gamma = jnp.log1p(-jnp.exp2(-5.0 - jnp.arange(heads, dtype=F32)))
    lg = jnp.broadcast_to(log_gamma[:, None, None], (heads, 1, HEAD_DIM))

    def col_spec(part):
        return pl.BlockSpec((rb, width), lambda r: (r, c0 + part))

    tab_spec = pl.BlockSpec((rb, HEAD_DIM), lambda r: (r, 0))
    return pl.pallas_call(
        functools.partial(_retention_kernel, heads=heads),
        grid=(s // rb,),
        in_specs=[col_spec(0), col_spec(1), col_spec(2), col_spec(3), tab_spec, tab_spec,
                  pl.BlockSpec((heads, 1, HEAD_DIM), lambda r: (0, 0, 0)),
                  pl.BlockSpec((1, width), lambda r: (0, 0))],
        out_specs=pl.BlockSpec((rb, width), lambda r: (r, 0)),
        out_shape=jax.ShapeDtypeStruct((s, width), BF16),
        scratch_shapes=[pltpu.VMEM((heads, HEAD_DIM, HEAD_DIM), F32),
                        pltpu.VMEM((heads, rb, rb), F32),
                        pltpu.VMEM((heads, rb, HEAD_DIM), F32),
                        pltpu.VMEM((heads, rb, HEAD_DIM), F32)],
        compiler_params=_params(("arbitrary",)),
        name="retention",
    )(proj, proj, proj, proj, cos, sin, lg, gn_w)


def _stickbreak_kernel(q_ref, k_ref, v_ref, o_ref, acc_ref, carry_ref, *, tq):
    tk = SB_KEYS
    hd = HEAD_DIM
    i = pl.program_id(1)
    scale = hd ** -0.5
    krow = lax.broadcasted_iota(jnp.int32, (tk, tk), 0)
    kcol = lax.broadcasted_iota(jnp.int32, (tk, tk), 1)
    neg_suffix = jnp.where(krow >= kcol, -1.0, 0.0).astype(BF16)
    neg_suffix2 = jnp.concatenate([neg_suffix, neg_suffix], axis=0)
    zeros = jnp.zeros((tk, hd), BF16)

    def block_diag(x):
        return jnp.concatenate([jnp.concatenate([x[:, :hd], zeros], axis=1),
                                jnp.concatenate([zeros, x[:, hd:]], axis=1)], axis=0)

    acc_ref[...] = jnp.zeros_like(acc_ref)
    carry_ref[...] = jnp.zeros_like(carry_ref)

    def key_rows(key_block):
        return pl.ds(pl.multiple_of(key_block * tk, tk), tk)

    def causal_mask(rows):
        key = lax.broadcasted_iota(jnp.int32, (rows, 2 * tk), 1) & (tk - 1)
        return key < lax.broadcasted_iota(jnp.int32, (rows, 2 * tk), 0)

    def scores(key_block, row0, rows):
        return _dot_nt(q_ref[row0:row0 + rows, :], block_diag(k_ref[key_rows(key_block), :])) * scale

    def suffix_sums(z, rows, masked):
        sp = jnp.maximum(z, 0.0) + jnp.log(1.0 + jnp.exp2(jnp.abs(z) * -LOG2E))
        if masked:
            sp = jnp.where(causal_mask(rows), sp, 0.0)
        hi = sp.astype(BF16)
        lo = (sp - hi.astype(F32)).astype(BF16)
        suffix = [_dot(jnp.concatenate([hi[:, h * tk:(h + 1) * tk], lo[:, h * tk:(h + 1) * tk]], axis=1),
                       neg_suffix2) for h in range(2)]
        totals = jnp.concatenate([jnp.broadcast_to(sfx[:, :1], (rows, hd)) for sfx in suffix], axis=1)
        return z + jnp.concatenate(suffix, axis=1), totals

    def attend(key_block, row0, rows, masked, arg, totals, valid=None):
        carry = carry_ref[row0:row0 + rows, :]
        c0, c1 = carry[:, :hd], carry[:, hd:]
        reps = tk // LANES
        w = jnp.exp(arg + jnp.concatenate([c0] * reps + [c1] * reps, axis=1))
        if masked:
            w = jnp.where(causal_mask(rows), w, 0.0)
        if valid is not None:
            w = jnp.where(valid, w, 0.0)
            totals = jnp.where(valid, totals, 0.0)
        acc_ref[row0:row0 + rows, :] += _dot(w.astype(BF16), block_diag(v_ref[key_rows(key_block), :]))
        carry_ref[row0:row0 + rows, :] = carry + totals

    def tile(key_block, row0, rows):
        attend(key_block, row0, rows, False, *suffix_sums(scores(key_block, row0, rows), rows, False))

    def alive(row0, rows):
        return jnp.max(carry_ref[row0:row0 + rows, :]) > SB_DEAD

    bands = tq // tk
    n_full = i * bands
    near = [(i * bands + b, b * tk, min(2 * tk, tq - b * tk), True) for b in range(bands - 1, -1, -1)]
    near.append((jnp.maximum(n_full - 1, 0), 0, tk, False))
    zs = [scores(kb, row0, rows) for kb, row0, rows, _ in near]
    sums = [suffix_sums(z, rows, masked) for z, (_, _, rows, masked) in zip(zs, near)]
    for (kb, row0, rows, masked), (arg, totals) in zip(near, sums):
        attend(kb, row0, rows, masked, arg, totals, None if masked else n_full > 0)
    for b in range(bands - 1, -1, -1):
        far0 = (b + 2) * tk
        if far0 < tq:
            @pl.when(alive(far0, tq - far0))
            def _():
                tile(i * bands + b, far0, tq - far0)

    @pl.when(jnp.logical_and(n_full > 0, alive(tk, tq - tk)))
    def _():
        tile(n_full - 1, tk, tq - tk)

    def body(n):
        tile(n_full - 1 - n, 0, tq)
        return n + 1

    lax.while_loop(lambda n: jnp.logical_and(n < n_full, alive(0, tq)), body, 1)
    o_ref[...] = acc_ref[...].astype(o_ref.dtype)


def stickbreak(proj, col0, heads):
    s = proj.shape[0]
    tq = min(SB_QUERIES, s)
    wide = 2 * HEAD_DIM
    c0 = col0 // wide
    pairs = heads // 2
    return pl.pallas_call(
        functools.partial(_stickbreak_kernel, tq=tq),
        grid=(pairs, s // tq),
        in_specs=[pl.BlockSpec((tq, wide), lambda g, i: (i, c0 + g)),
                  pl.BlockSpec((s, wide), lambda g, i: (0, c0 + pairs + g)),
                  pl.BlockSpec((s, wide), lambda g, i: (0, c0 + 2 * pairs + g))],
        out_specs=pl.BlockSpec((tq, wide), lambda g, i: (i, g)),
        out_shape=jax.ShapeDtypeStruct((s, heads * HEAD_DIM), BF16),
        scratch_shapes=[pltpu.VMEM((tq, wide), F32), pltpu.VMEM((tq, wide), F32)],
        compiler_params=_params(("parallel", "arbitrary")),
        name="stickbreak",
    )(proj, proj, proj)


def _ssd_kernel(xbc_ref, z_ref, dt_ref, cw_ref, cb_ref, dtb_ref, alog_ref, dskip_ref, nw_ref, o_ref,
                win_ref, state_ref, y_ref, *, width, heads):
    lb = SSD_BLOCK
    heads_per_group = heads // SSD_GROUPS

    @pl.when(pl.program_id(0) == 0)
    def _():
        state_ref[...] = jnp.zeros_like(state_ref)
        win_ref[...] = jnp.zeros_like(win_ref)

    u = xbc_ref[...]
    both = jnp.concatenate([u, win_ref[...]], axis=0)
    t = lax.broadcasted_iota(jnp.int32, (lb, 2 * lb), 0)
    src = lax.broadcasted_iota(jnp.int32, (lb, 2 * lb), 1)
    conv = cb_ref[...] + cw_ref[SSD_CONV - 1:SSD_CONV, :] * u.astype(F32)
    for back in range(1, SSD_CONV):
        wanted = jnp.where(t >= back, t - back, 2 * lb + t - back)
        shifted = _dot(jnp.where(src == wanted, 1.0, 0.0).astype(BF16), both)
        conv = conv + cw_ref[SSD_CONV - 1 - back:SSD_CONV - back, :] * shifted
    win_ref[...] = u
    act = _silu(conv)
    xs = act[:, :width]
    gn = SSD_GROUPS * SSD_STATE
    bm = act[:, width:width + gn]
    cm = act[:, width + gn:width + 2 * gn]

    dt = _softplus(dt_ref[...] + dtb_ref[...])
    a = dt * (-jnp.exp(alog_ref[...]))
    row = lax.broadcasted_iota(jnp.int32, (lb, lb), 0)
    col = lax.broadcasted_iota(jnp.int32, (lb, lb), 1)
    causal = col <= row
    tri = jnp.where(causal, 1.0, 0.0).astype(BF16)
    a_hi, a_mid, a_lo = _split3(a)
    acum = _dot(tri, a_hi) + _dot(tri, a_mid) + _dot(tri, a_lo)
    acum_t = jnp.transpose(acum)
    lane = lax.broadcasted_iota(jnp.int32, (lb, LANES), 1)
    low_half = lane < SSD_HEAD_DIM

    for g in range(SSD_GROUPS):
        bg = bm[:, g * SSD_STATE:(g + 1) * SSD_STATE]
        cg = cm[:, g * SSD_STATE:(g + 1) * SSD_STATE].astype(BF16)
        cb = jnp.where(causal, _dot_nt(cg, bg.astype(BF16)), 0.0)
        bg_t = jnp.transpose(bg).astype(BF16)
        for pp in range(heads_per_group // 2):
            p = g * (heads_per_group // 2) + pp
            e0, e1 = 2 * p, 2 * p + 1
            m_parts, acol_parts, dt_parts = [], [], []
            for e in (e0, e1):
                acol = jnp.broadcast_to(acum[:, e:e + 1], (lb, lb))
                seg = acol - acum_t[e:e + 1, :]
                decay = jnp.exp(jnp.where(causal, seg, 0.0))
                m_parts.append((cb * decay).astype(BF16))
                acol_parts.append(acol)
                dt_parts.append(jnp.broadcast_to(dt[:, e:e + 1], (lb, LANES)))
            acol_pair = jnp.where(low_half, acol_parts[0], acol_parts[1])
            dt_pair = jnp.where(low_half, dt_parts[0], dt_parts[1])
            x_pair = xs[:, p * LANES:(p + 1) * LANES]
            xdt = x_pair * dt_pair
            xdt_lo = jnp.where(low_half, xdt, 0.0).astype(BF16)
            xdt_hi = jnp.where(low_half, 0.0, xdt).astype(BF16)
            y = _dot(jnp.concatenate(m_parts, axis=1), jnp.concatenate([xdt_lo, xdt_hi], axis=0))
            state = state_ref[p]
            y = y + _dot(cg, state.astype(BF16)) * jnp.exp(acol_pair)
            last = acol_pair[lb - 1:lb, :]
            to_end = jnp.exp(last - acol_pair)
            state_ref[p] = state * jnp.exp(last) + _dot(bg_t, (xdt * to_end).astype(BF16))
            y = y + x_pair * dskip_ref[:, p * LANES:(p + 1) * LANES]
            y_ref[:, p * LANES:(p + 1) * LANES] = y

    yz = y_ref[...] * _silu(z_ref[...].astype(F32))
    ms = jnp.mean(yz * yz, axis=-1, keepdims=True)
    o_ref[...] = (yz * lax.rsqrt(ms + NORM_EPS) * nw_ref[...]).astype(o_ref.dtype)


def ssd(proj, z_col0, xbc_col0, dt_raw, conv_w, conv_b, dt_bias, a_log, d_skip, norm_w, width, heads):
    s = proj.shape[0]
    lb = SSD_BLOCK
    conv_dim = width + 2 * SSD_GROUPS * SSD_STATE
    row_vec = lambda n: pl.BlockSpec((1, n), lambda r: (0, 0))
    kernel = functools.partial(_ssd_kernel, width=width, heads=heads)
    return pl.pallas_call(
        kernel,
        grid=(s // lb,),
        in_specs=[pl.BlockSpec((lb, conv_dim), lambda r: (r, xbc_col0 // conv_dim)),
                  pl.BlockSpec((lb, width), lambda r: (r, z_col0 // width)),
                  pl.BlockSpec((lb, LANES), lambda r: (r, 0)),
                  pl.BlockSpec((SSD_CONV, conv_dim), lambda r: (0, 0)),
                  row_vec(conv_dim), row_vec(LANES), row_vec(LANES), row_vec(width), row_vec(width)],
        out_specs=pl.BlockSpec((lb, width), lambda r: (r, 0)),
        out_shape=jax.ShapeDtypeStruct((s, width), BF16),
        scratch_shapes=[pltpu.VMEM((lb, conv_dim), BF16),
                        pltpu.VMEM((heads // 2, SSD_STATE, LANES), F32),
                        pltpu.VMEM((lb, width), F32)],
        compiler_params=_params(("arbitrary",)),
        name="ssd",
    )(proj, proj, dt_raw, conv_w, conv_b, dt_bias, a_log, d_skip, norm_w)


def _pad_lanes(v):
    return jnp.pad(v.astype(F32), (0, LANES - v.shape[0])).reshape(1, LANES)


def kernel(x, positions, norm_mix_pre, norm_mix_post, norm_ffn_pre, norm_ffn_post, w_in, b_gate, ret_gn_w, ssd_conv_w, ssd_conv_b, ssd_dt_bias, ssd_a_log, ssd_d, ssd_norm_w, w_branch_ret, w_branch_sb, w_branch_ssd, w_out, ffn_w_gate, ffn_w_up, ffn_w_down):
    b, s, d = x.shape
    assert b == 1
    depth = w_in.shape[0]
    width = w_branch_ret.shape[1]
    ret_heads = width // HEAD_DIM
    sb_heads = width // HEAD_DIM
    ssd_heads = ssd_d.shape[1]
    conv_dim = ssd_conv_w.shape[2]
    dt_col = 4 * width + 3 * width + width + conv_dim
    ret_col, sb_col = 0, 4 * width
    z_col, xbc_col = 7 * width, 8 * width
    assert dt_col % LANES == 0 and ssd_heads <= LANES

    n_in = w_in.shape[2]
    w_in_t = jnp.swapaxes(w_in, 1, 2).reshape(depth * n_in, d)
    w_ret, w_sb, w_ssd = (w.astype(BF16) for w in (w_branch_ret, w_branch_sb, w_branch_ssd))
    w_out_b = w_out.astype(BF16)
    w_down_b = ffn_w_down.astype(BF16)

    xf = x.reshape(s, d)
    cos, sin = rotary_tables(positions.reshape(s, 1), 1024)

    h = rms_norm(xf, norm_mix_pre[0].reshape(1, d), 1024)
    for l in range(depth):
        proj = matmul_nt(h, w_in_t, l * n_in, dt_col, BF16, 2048, 512)
        dt_raw = matmul_nt(h, w_in_t, l * n_in + dt_col, LANES, F32, 2048, LANES)
        gate_logits = matmul_nt(h, w_in_t, l * n_in + dt_col + ssd_heads, 3 * d, BF16, 2048, 512)

        y_ret = retention(proj, ret_col, cos, sin, ret_gn_w[l].reshape(1, width), ret_heads)
        y_sb = stickbreak(proj, sb_col, sb_heads)
        y_ssd = ssd(proj, z_col, xbc_col, dt_raw, ssd_conv_w[l], ssd_conv_b[l].reshape(1, conv_dim),
                    _pad_lanes(ssd_dt_bias[l]), _pad_lanes(ssd_a_log[l]),
                    jnp.repeat(ssd_d[l], SSD_HEAD_DIM).reshape(1, width),
                    ssd_norm_w[l].reshape(1, width), width, ssd_heads)

        merged = gated_merge(y_ret, y_sb, y_ssd, w_ret, w_sb, w_ssd, l, gate_logits,
                             b_gate[l].reshape(1, 3 * d), 1024, 512)
        xf, h = matmul_norm_res(merged, w_out_b, l, xf, norm_mix_post[l].reshape(1, d),
                                norm_ffn_pre[l].reshape(1, d), 512, d)

        act = swiglu(h, ffn_w_gate, ffn_w_up, l, 2048, 512)
        g_next = norm_mix_pre[l + 1].reshape(1, d) if l + 1 < depth else None
        xf, h = matmul_norm_res(act, w_down_b, l, xf, norm_ffn_post[l].reshape(1, d), g_next, 512, d)

    return xf.reshape(b, s, d)
```

```python
import functools
import math

import jax
import jax.numpy as jnp
from jax import lax
from jax.experimental import pallas as pl
from jax.experimental.pallas import tpu as pltpu

F32 = jnp.float32
BF16 = jnp.bfloat16

NORM_EPS = 1e-6
ROPE_BASE = 10000.0
LOG2E = math.log2(math.e)
LANES = 128
SUBLANES = 8
HEAD_DIM = 128
RET_CHUNK = 64
RET_BLOCK = 256
SB_QUERIES = 1024
SB_KEYS = 256
SB_DEAD = -104.0
SSD_BLOCK = 128
SSD_HEAD_DIM = 64
SSD_GROUPS = 4
SSD_STATE = 128
SSD_CONV = 4
STREAM_ROWS = 2048
STREAM_COLS = 512
RESIDENT_ROWS = 512
TABLE_ROWS = 1024
VMEM_LIMIT = 56 * 1024 * 1024


VMEM_LIMIT_RESIDENT = 60 * 1024 * 1024


def _params(sem, vmem_limit=VMEM_LIMIT):
    return pltpu.CompilerParams(dimension_semantics=sem, vmem_limit_bytes=vmem_limit)


def _sigmoid(x):
    return 1.0 / (1.0 + jnp.exp(-x))


def _silu(x):
    return x * _sigmoid(x)


def _log1p(e):
    u = 1.0 + e
    d = u - 1.0
    return jnp.where(d == 0.0, e, jnp.log(u) * (e / jnp.where(d == 0.0, 1.0, d)))


def _softplus(x):
    return jnp.maximum(x, 0.0) + _log1p(jnp.exp(-jnp.abs(x)))


def _split3(x):
    hi = x.astype(BF16)
    r1 = x - hi.astype(F32)
    mid = r1.astype(BF16)
    lo = (r1 - mid.astype(F32)).astype(BF16)
    return hi, mid, lo


def _dot(a, b):
    return jnp.dot(a, b, preferred_element_type=F32)


def _dot_nt(a, b):
    return lax.dot_general(a, b, (((1,), (1,)), ((), ())), preferred_element_type=F32)


def _rms_norm_kernel(x_ref, g_ref, h_ref):
    x = x_ref[...]
    ms = jnp.mean(x * x, axis=-1, keepdims=True)
    h_ref[...] = (x * lax.rsqrt(ms + NORM_EPS) * g_ref[...]).astype(h_ref.dtype)


def rms_norm(x, g, tm):
    m, d = x.shape
    return pl.pallas_call(
        _rms_norm_kernel,
        grid=(m // tm,),
        in_specs=[pl.BlockSpec((tm, d), lambda i: (i, 0)), pl.BlockSpec((1, d), lambda i: (0, 0))],
        out_specs=pl.BlockSpec((tm, d), lambda i: (i, 0)),
        out_shape=jax.ShapeDtypeStruct((m, d), BF16),
        compiler_params=_params(("parallel",)),
        name="rms_norm",
    )(x, g)


def _matmul_nt_kernel(h_ref, wt_ref, o_ref):
    o_ref[...] = _dot_nt(h_ref[...], wt_ref[...].astype(BF16)).astype(o_ref.dtype)


def matmul_nt(h, wt, row0, n, out_dtype, tm, tn):
    m, d = h.shape
    assert row0 % SUBLANES == 0 and tn % SUBLANES == 0
    return pl.pallas_call(
        _matmul_nt_kernel,
        grid=(m // tm, n // tn),
        in_specs=[pl.BlockSpec((tm, d), lambda i, j: (i, 0)),
                  pl.BlockSpec((pl.Element(tn), pl.Element(d)),
                               lambda i, j: (pl.multiple_of(row0 + j * tn, SUBLANES), 0))],
        out_specs=pl.BlockSpec((tm, tn), lambda i, j: (i, j)),
        out_shape=jax.ShapeDtypeStruct((m, n), out_dtype),
        compiler_params=_params(("parallel", "arbitrary")),
        name="matmul_nt",
    )(h, wt)


def _swiglu_kernel(h_ref, wg_ref, wu_ref, o_ref):
    wg = wg_ref[...].astype(BF16)
    wu = wu_ref[...].astype(BF16)
    half = h_ref.shape[0] // 2
    for r in range(2):
        rows = slice(r * half, (r + 1) * half)
        h = h_ref[rows, :]
        o_ref[rows, :] = (_silu(_dot(h, wg)) * _dot(h, wu)).astype(o_ref.dtype)


def swiglu(h, wg_stack, wu_stack, layer, tm, tn):
    m, d = h.shape
    n = wg_stack.shape[2]
    w_spec = pl.BlockSpec((None, d, tn), lambda i, j: (layer, 0, j))
    return pl.pallas_call(
        _swiglu_kernel,
        grid=(m // tm, n // tn),
        in_specs=[pl.BlockSpec((tm, d), lambda i, j: (i, 0)), w_spec, w_spec],
        out_specs=pl.BlockSpec((tm, tn), lambda i, j: (i, j)),
        out_shape=jax.ShapeDtypeStruct((m, n), BF16),
        compiler_params=_params(("parallel", "arbitrary")),
        name="swiglu",
    )(h, wg_stack, wu_stack)


def _matmul_norm_res_kernel(a_ref, w_ref, x_ref, g_ref, *rest):
    emit_next = len(rest) == 4
    gn_ref, o_ref, h_ref, f_ref = rest if emit_next else (None, rest[0], None, rest[1])
    j = pl.program_id(1)
    nj, _, tn = f_ref.shape
    f_ref[j] = _dot(a_ref[...], w_ref[...])

    @pl.when(j == nj - 1)
    def _():
        ss = jnp.zeros((f_ref.shape[1], 1), F32)
        for c in range(nj):
            f = f_ref[c]
            ss = ss + jnp.sum(f * f, axis=-1, keepdims=True)
        inv = lax.rsqrt(ss / (nj * tn) + NORM_EPS)
        ss = jnp.zeros((f_ref.shape[1], 1), F32)
        for c in range(nj):
            cols = slice(c * tn, (c + 1) * tn)
            y = x_ref[:, cols] + f_ref[c] * inv * g_ref[:, cols]
            o_ref[:, cols] = y
            ss = ss + jnp.sum(y * y, axis=-1, keepdims=True)
        if emit_next:
            inv = lax.rsqrt(ss / (nj * tn) + NORM_EPS)
            for c in range(nj):
                cols = slice(c * tn, (c + 1) * tn)
                h_ref[:, cols] = (o_ref[:, cols] * inv * gn_ref[:, cols]).astype(h_ref.dtype)


def matmul_norm_res(a, w_stack, layer, x, g, g_next, tm, tn):
    m, kdim = a.shape
    d = w_stack.shape[2]
    row_spec = pl.BlockSpec((tm, d), lambda i, j: (i, 0))
    vec_spec = pl.BlockSpec((1, d), lambda i, j: (0, 0))
    emit_next = g_next is not None
    outs = pl.pallas_call(
        _matmul_norm_res_kernel,
        grid=(m // tm, d // tn),
        in_specs=[pl.BlockSpec((tm, kdim), lambda i, j: (i, 0)),
                  pl.BlockSpec((None, kdim, tn), lambda i, j: (layer, 0, j),
                               **({"pipeline_mode": pl.Buffered(1)} if tn == d else {})),
                  row_spec, vec_spec] + ([vec_spec] if emit_next else []),
        out_specs=[row_spec] + ([row_spec] if emit_next else []),
        out_shape=[jax.ShapeDtypeStruct((m, d), F32)] + ([jax.ShapeDtypeStruct((m, d), BF16)] if emit_next else []),
        scratch_shapes=[pltpu.VMEM((d // tn, tm, tn), F32)],
        compiler_params=_params(("parallel", "arbitrary"), VMEM_LIMIT_RESIDENT if tn == d else VMEM_LIMIT),
        name="matmul_norm_res",
    )(a, w_stack, x, g, *([g_next] if emit_next else []))
    return (outs[0], outs[1]) if emit_next else (outs[0], None)


def _gated_merge_kernel(yr_ref, ys_ref, yd_ref, wr_ref, ws_ref, wd_ref,
                        gr_ref, gs_ref, gd_ref, br_ref, bs_ref, bd_ref, o_ref):
    def branch(y_ref, w_ref, gl_ref, b_ref):
        gate = _sigmoid(gl_ref[...].astype(F32) + b_ref[...])
        return gate * _dot(y_ref[...], w_ref[...])

    o_ref[...] = (branch(yr_ref, wr_ref, gr_ref, br_ref)
                  + branch(ys_ref, ws_ref, gs_ref, bs_ref)
                  + branch(yd_ref, wd_ref, gd_ref, bd_ref)).astype(o_ref.dtype)


def gated_merge(y_ret, y_sb, y_ssd, w_ret, w_sb, w_ssd, layer, gate_logits, b_gate, tm, tn):
    m, width = y_ret.shape
    d = w_ret.shape[2]
    nb = d // tn
    y_spec = pl.BlockSpec((tm, width), lambda i, j: (i, 0))
    w_spec = pl.BlockSpec((None, width, tn), lambda i, j: (layer, 0, j),
                          **({"pipeline_mode": pl.Buffered(1)} if tn == d else {}))

    def gate_spec(b):
        return pl.BlockSpec((tm, tn), lambda i, j: (i, b * nb + j))

    def bias_spec(b):
        return pl.BlockSpec((1, tn), lambda i, j: (0, b * nb + j))

    return pl.pallas_call(
        _gated_merge_kernel,
        grid=(m // tm, nb),
        in_specs=[y_spec, y_spec, y_spec, w_spec, w_spec, w_spec,
                  gate_spec(0), gate_spec(1), gate_spec(2),
                  bias_spec(0), bias_spec(1), bias_spec(2)],
        out_specs=pl.BlockSpec((tm, tn), lambda i, j: (i, j)),
        out_shape=jax.ShapeDtypeStruct((m, d), BF16),
        compiler_params=_params(("parallel", "arbitrary")),
        name="gated_merge",
    )(y_ret, y_sb, y_ssd, w_ret, w_sb, w_ssd, gate_logits, gate_logits, gate_logits, b_gate, b_gate, b_gate)


def _rotary_kernel(pos_ref, freq_ref, cos_ref, sin_ref):
    ang = pos_ref[...].astype(F32) * freq_ref[...]
    lane = lax.broadcasted_iota(jnp.int32, ang.shape, 1)
    cos_ref[...] = jnp.cos(ang)
    sin_ref[...] = jnp.where(lane < HEAD_DIM // 2, -jnp.sin(ang), jnp.sin(ang))


def rotary_tables(positions_col, ts):
    s = positions_col.shape[0]
    half = HEAD_DIM // 2
    inv_freq = ROPE_BASE ** (-2.0 * jnp.arange(half, dtype=F32) / HEAD_DIM)
    freq = jnp.concatenate([inv_freq, inv_freq]).reshape(1, HEAD_DIM)
    return pl.pallas_call(
        _rotary_kernel,
        grid=(s // ts,),
        in_specs=[pl.BlockSpec((ts, 1), lambda i: (i, 0)),
                  pl.BlockSpec((1, HEAD_DIM), lambda i: (0, 0))],
        out_specs=[pl.BlockSpec((ts, HEAD_DIM), lambda i: (i, 0)),
                   pl.BlockSpec((ts, HEAD_DIM), lambda i: (i, 0))],
        out_shape=[jax.ShapeDtypeStruct((s, HEAD_DIM), F32),
                   jax.ShapeDtypeStruct((s, HEAD_DIM), F32)],
        compiler_params=_params(("parallel",)),
        name="rotary_tables",
    )(positions_col, freq)


def _retention_kernel(q_ref, k_ref, v_ref, g_ref, cos_ref, sin_ref, lg_ref, gn_ref, o_ref,
                      state_ref, decay_ref, qd_ref, kd_ref, *, heads):
    rb = RET_BLOCK

    @pl.when(pl.program_id(0) == 0)
    def _():
        state_ref[...] = jnp.zeros_like(state_ref)
        t = lax.broadcasted_iota(jnp.int32, (rb, rb), 0)
        s = lax.broadcasted_iota(jnp.int32, (rb, rb), 1)
        dist = jnp.abs(t - s).astype(F32)
        visible = (s // RET_CHUNK) <= (t // RET_CHUNK)
        row = lax.broadcasted_iota(jnp.int32, (rb, HEAD_DIM), 0).astype(F32)
        for h in range(heads):
            lg = lg_ref[h]
            decay_ref[h] = jnp.where(visible, jnp.exp(lg[:, :1] * dist), 0.0)
            qd_ref[h] = jnp.exp(lg * (row + 1.0))
            kd_ref[h] = jnp.exp(lg * (rb - 1.0 - row))

    cos = cos_ref[...]
    sin = sin_ref[...]

    def rotate(t):
        return t * cos + pltpu.roll(t, HEAD_DIM // 2, 1) * sin

    for h in range(heads):
        lanes = slice(h * HEAD_DIM, (h + 1) * HEAD_DIM)
        q = rotate(q_ref[:, lanes].astype(F32))
        k = rotate(k_ref[:, lanes].astype(F32)) * (HEAD_DIM ** -0.5)
        v = v_ref[:, lanes]
        scores = _dot_nt(q.astype(BF16), k.astype(BF16)) * decay_ref[h]
        o = _dot(scores.astype(BF16), v)
        state = state_ref[h]
        o = o + _dot((q * qd_ref[h]).astype(BF16), state.astype(BF16))
        kdt = jnp.transpose(k * kd_ref[h]).astype(BF16)
        block_decay = jnp.exp(lg_ref[h] * float(rb))
        state_ref[h] = state * block_decay + _dot(kdt, v)

        mu = jnp.mean(o, axis=-1, keepdims=True)
        oc = o - mu
        var = jnp.mean(oc * oc, axis=-1, keepdims=True)
        o = oc * lax.rsqrt(var + NORM_EPS) * gn_ref[:, lanes]
        o_ref[:, lanes] = (o * _silu(g_ref[:, lanes].astype(F32))).astype(o_ref.dtype)


def retention(proj, col0, cos, sin, gn_w, heads):
    s = proj.shape[0]
    rb = RET_BLOCK
    width = heads * HEAD_DIM
    c0 = col0 // width
    log_gamma = jnp.log1p(-jnp.exp2(-5.0 - jnp.arange(heads, dtype=F32)))
    lg = jnp.broadcast_to(log_gamma[:, None, None], (heads, 1, HEAD_DIM))

    def col_spec(part):
        return pl.BlockSpec((rb, width), lambda r: (r, c0 + part))

    tab_spec = pl.BlockSpec((rb, HEAD_DIM), lambda r: (r, 0))
    return pl.pallas_call(
        functools.partial(_retention_kernel, heads=heads),
        grid=(s // rb,),
        in_specs=[col_spec(0), col_spec(1), col_spec(2), col_spec(3), tab_spec, tab_spec,
                  pl.BlockSpec((heads, 1, HEAD_DIM), lambda r: (0, 0, 0)),
                  pl.BlockSpec((1, width), lambda r: (0, 0))],
        out_specs=pl.BlockSpec((rb, width), lambda r: (r, 0)),
        out_shape=jax.ShapeDtypeStruct((s, width), BF16),
        scratch_shapes=[pltpu.VMEM((heads, HEAD_DIM, HEAD_DIM), F32),
                        pltpu.VMEM((heads, rb, rb), F32),
                        pltpu.VMEM((heads, rb, HEAD_DIM), F32),
                        pltpu.VMEM((heads, rb, HEAD_DIM), F32)],
        compiler_params=_params(("arbitrary",)),
        name="retention",
    )(proj, proj, proj, proj, cos, sin, lg, gn_w)


def _stickbreak_kernel(q_ref, k_ref, v_ref, o_ref, acc_ref, carry_ref, *, tq):
    tk = SB_KEYS
    hd = HEAD_DIM
    i = pl.program_id(1)
    scale = hd ** -0.5
    krow = lax.broadcasted_iota(jnp.int32, (tk, tk), 0)
    kcol = lax.broadcasted_iota(jnp.int32, (tk, tk), 1)
    neg_suffix = jnp.where(krow >= kcol, -1.0, 0.0).astype(BF16)
    neg_suffix2 = jnp.concatenate([neg_suffix, neg_suffix], axis=0)
    zeros = jnp.zeros((tk, hd), BF16)

    def block_diag(x):
        return jnp.concatenate([jnp.concatenate([x[:, :hd], zeros], axis=1),
                                jnp.concatenate([zeros, x[:, hd:]], axis=1)], axis=0)

    acc_ref[...] = jnp.zeros_like(acc_ref)
    carry_ref[...] = jnp.zeros_like(carry_ref)

    def key_rows(key_block):
        return pl.ds(pl.multiple_of(key_block * tk, tk), tk)

    def causal_mask(rows):
        key = lax.broadcasted_iota(jnp.int32, (rows, 2 * tk), 1) & (tk - 1)
        return key < lax.broadcasted_iota(jnp.int32, (rows, 2 * tk), 0)

    def scores(key_block, row0, rows):
        return _dot_nt(q_ref[row0:row0 + rows, :], block_diag(k_ref[key_rows(key_block), :])) * scale

    def suffix_sums(z, rows, masked):
        sp = jnp.maximum(z, 0.0) + jnp.log(1.0 + jnp.exp2(jnp.abs(z) * -LOG2E))
        if masked:
            sp = jnp.where(causal_mask(rows), sp, 0.0)
        hi = sp.astype(BF16)
        lo = (sp - hi.astype(F32)).astype(BF16)
        suffix = [_dot(jnp.concatenate([hi[:, h * tk:(h + 1) * tk], lo[:, h * tk:(h + 1) * tk]], axis=1),
                       neg_suffix2) for h in range(2)]
        totals = jnp.concatenate([jnp.broadcast_to(sfx[:, :1], (rows, hd)) for sfx in suffix], axis=1)
        return z + jnp.concatenate(suffix, axis=1), totals

    def attend(key_block, row0, rows, masked, arg, totals, valid=None):
        carry = carry_ref[row0:row0 + rows, :]
        c0, c1 = carry[:, :hd], carry[:, hd:]
        reps = tk // LANES
        w = jnp.exp(arg + jnp.concatenate([c0] * reps + [c1] * reps, axis=1))
        if masked:
            w = jnp.where(causal_mask(rows), w, 0.0)
        if valid is not None:
            w = jnp.where(valid, w, 0.0)
            totals = jnp.where(valid, totals, 0.0)
        acc_ref[row0:row0 + rows, :] += _dot(w.astype(BF16), block_diag(v_ref[key_rows(key_block), :]))
        carry_ref[row0:row0 + rows, :] = carry + totals

    def tile(key_block, row0, rows):
        attend(key_block, row0, rows, False, *suffix_sums(scores(key_block, row0, rows), rows, False))

    def alive(row0, rows):
        return jnp.max(carry_ref[row0:row0 + rows, :]) > SB_DEAD

    bands = tq // tk
    n_full = i * bands
    near = [(i * bands + b, b * tk, min(2 * tk, tq - b * tk), True) for b in range(bands - 1, -1, -1)]
    near.append((jnp.maximum(n_full - 1, 0), 0, tk, False))
    zs = [scores(kb, row0, rows) for kb, row0, rows, _ in near]
    sums = [suffix_sums(z, rows, masked) for z, (_, _, rows, masked) in zip(zs, near)]
    for (kb, row0, rows, masked), (arg, totals) in zip(near, sums):
        attend(kb, row0, rows, masked, arg, totals, None if masked else n_full > 0)
    for b in range(bands - 1, -1, -1):
        far0 = (b + 2) * tk
        if far0 < tq:
            @pl.when(alive(far0, tq - far0))
            def _():
                tile(i * bands + b, far0, tq - far0)

    @pl.when(jnp.logical_and(n_full > 0, alive(tk, tq - tk)))
    def _():
        tile(n_full - 1, tk, tq - tk)

    def body(n):
        tile(n_full - 1 - n, 0, tq)
        return n + 1

    lax.while_loop(lambda n: jnp.logical_and(n < n_full, alive(0, tq)), body, 1)
    o_ref[...] = acc_ref[...].astype(o_ref.dtype)


def stickbreak(proj, col0, heads):
    s = proj.shape[0]
    tq = min(SB_QUERIES, s)
    wide = 2 * HEAD_DIM
    c0 = col0 // wide
    pairs = heads // 2
    return pl.pallas_call(
        functools.partial(_stickbreak_kernel, tq=tq),
        grid=(pairs, s // tq),
        in_specs=[pl.BlockSpec((tq, wide), lambda g, i: (i, c0 + g)),
                  pl.BlockSpec((s, wide), lambda g, i: (0, c0 + pairs + g)),
                  pl.BlockSpec((s, wide), lambda g, i: (0, c0 + 2 * pairs + g))],
        out_specs=pl.BlockSpec((tq, wide), lambda g, i: (i, g)),
        out_shape=jax.ShapeDtypeStruct((s, heads * HEAD_DIM), BF16),
        scratch_shapes=[pltpu.VMEM((tq, wide), F32), pltpu.VMEM((tq, wide), F32)],
        compiler_params=_params(("parallel", "arbitrary")),
        name="stickbreak",
    )(proj, proj, proj)


def _ssd_kernel(xbc_ref, z_ref, dt_ref, cw_ref, cb_ref, dtb_ref, alog_ref, dskip_ref, nw_ref, o_ref,
                win_ref, state_ref, y_ref, *, width, heads):
    lb = SSD_BLOCK
    heads_per_group = heads // SSD_GROUPS

    @pl.when(pl.program_id(0) == 0)
    def _():
        state_ref[...] = jnp.zeros_like(state_ref)
        win_ref[...] = jnp.zeros_like(win_ref)

    u = xbc_ref[...]
    both = jnp.concatenate([u, win_ref[...]], axis=0)
    t = lax.broadcasted_iota(jnp.int32, (lb, 2 * lb), 0)
    src = lax.broadcasted_iota(jnp.int32, (lb, 2 * lb), 1)
    conv = cb_ref[...] + cw_ref[SSD_CONV - 1:SSD_CONV, :] * u.astype(F32)
    for back in range(1, SSD_CONV):
        wanted = jnp.where(t >= back, t - back, 2 * lb + t - back)
        shifted = _dot(jnp.where(src == wanted, 1.0, 0.0).astype(BF16), both)
        conv = conv + cw_ref[SSD_CONV - 1 - back:SSD_CONV - back, :] * shifted
    win_ref[...] = u
    act = _silu(conv)
    xs = act[:, :width]
    gn = SSD_GROUPS * SSD_STATE
    bm = act[:, width:width + gn]
    cm = act[:, width + gn:width + 2 * gn]

    dt = _softplus(dt_ref[...] + dtb_ref[...])
    a = dt * (-jnp.exp(alog_ref[...]))
    row = lax.broadcasted_iota(jnp.int32, (lb, lb), 0)
    col = lax.broadcasted_iota(jnp.int32, (lb, lb), 1)
    causal = col <= row
    tri = jnp.where(causal, 1.0, 0.0).astype(BF16)
    a_hi, a_mid, a_lo = _split3(a)
    acum = _dot(tri, a_hi) + _dot(tri, a_mid) + _dot(tri, a_lo)
    acum_t = jnp.transpose(acum)
    lane = lax.broadcasted_iota(jnp.int32, (lb, LANES), 1)
    low_half = lane < SSD_HEAD_DIM

    for g in range(SSD_GROUPS):
        bg = bm[:, g * SSD_STATE:(g + 1) * SSD_STATE]
        cg = cm[:, g * SSD_STATE:(g + 1) * SSD_STATE].astype(BF16)
        cb = jnp.where(causal, _dot_nt(cg, bg.astype(BF16)), 0.0)
        bg_t = jnp.transpose(bg).astype(BF16)
        for pp in range(heads_per_group // 2):
            p = g * (heads_per_group // 2) + pp
            e0, e1 = 2 * p, 2 * p + 1
            m_parts, acol_parts, dt_parts = [], [], []
            for e in (e0, e1):
                acol = jnp.broadcast_to(acum[:, e:e + 1], (lb, lb))
                seg = acol - acum_t[e:e + 1, :]
                decay = jnp.exp(jnp.where(causal, seg, 0.0))
                m_parts.append((cb * decay).astype(BF16))
                acol_parts.append(acol)
                dt_parts.append(jnp.broadcast_to(dt[:, e:e + 1], (lb, LANES)))
            acol_pair = jnp.where(low_half, acol_parts[0], acol_parts[1])
            dt_pair = jnp.where(low_half, dt_parts[0], dt_parts[1])
            x_pair = xs[:, p * LANES:(p + 1) * LANES]
            xdt = x_pair * dt_pair
            xdt_lo = jnp.where(low_half, xdt, 0.0).astype(BF16)
            xdt_hi = jnp.where(low_half, 0.0, xdt).astype(BF16)
            y = _dot(jnp.concatenate(m_parts, axis=1), jnp.concatenate([xdt_lo, xdt_hi], axis=0))
            state = state_ref[p]
            y = y + _dot(cg, state.astype(BF16)) * jnp.exp(acol_pair)
            last = acol_pair[lb - 1:lb, :]
            to_end = jnp.exp(last - acol_pair)
            state_ref[p] = state * jnp.exp(last) + _dot(bg_t, (xdt * to_end).astype(BF16))
            y = y + x_pair * dskip_ref[:, p * LANES:(p + 1) * LANES]
            y_ref[:, p * LANES:(p + 1) * LANES] = y

    yz = y_ref[...] * _silu(z_ref[...].astype(F32))
    ms = jnp.mean(yz * yz, axis=-1, keepdims=True)
    o_ref[...] = (yz * lax.rsqrt(ms + NORM_EPS) * nw_ref[...]).astype(o_ref.dtype)


def ssd(proj, z_col0, xbc_col0, dt_raw, conv_w, conv_b, dt_bias, a_log, d_skip, norm_w, width, heads):
    s = proj.shape[0]
    lb = SSD_BLOCK
    conv_dim = width + 2 * SSD_GROUPS * SSD_STATE
    row_vec = lambda n: pl.BlockSpec((1, n), lambda r: (0, 0))
    kernel = functools.partial(_ssd_kernel, width=width, heads=heads)
    return pl.pallas_call(
        kernel,
        grid=(s // lb,),
        in_specs=[pl.BlockSpec((lb, conv_dim), lambda r: (r, xbc_col0 // conv_dim)),
                  pl.BlockSpec((lb, width), lambda r: (r, z_col0 // width)),
                  pl.BlockSpec((lb, LANES), lambda r: (r, 0)),
                  pl.BlockSpec((SSD_CONV, conv_dim), lambda r: (0, 0)),
                  row_vec(conv_dim), row_vec(LANES), row_vec(LANES), row_vec(width), row_vec(width)],
        out_specs=pl.BlockSpec((lb, width), lambda r: (r, 0)),
        out_shape=jax.ShapeDtypeStruct((s, width), BF16),
        scratch_shapes=[pltpu.VMEM((lb, conv_dim), BF16),
                        pltpu.VMEM((heads // 2, SSD_STATE, LANES), F32),
                        pltpu.VMEM((lb, width), F32)],
        compiler_params=_params(("arbitrary",)),
        name="ssd",
    )(proj, proj, dt_raw, conv_w, conv_b, dt_bias, a_log, d_skip, norm_w)


def _pad_lanes(v):
    return jnp.pad(v.astype(F32), (0, LANES - v.shape[0])).reshape(1, LANES)


def kernel(x, positions, norm_mix_pre, norm_mix_post, norm_ffn_pre, norm_ffn_post, w_in, b_gate, ret_gn_w, ssd_conv_w, ssd_conv_b, ssd_dt_bias, ssd_a_log, ssd_d, ssd_norm_w, w_branch_ret, w_branch_sb, w_branch_ssd, w_out, ffn_w_gate, ffn_w_up, ffn_w_down):
    b, s, d = x.shape
    assert b == 1
    depth = w_in.shape[0]
    width = w_branch_ret.shape[1]
    ret_heads = width // HEAD_DIM
    sb_heads = width // HEAD_DIM
    ssd_heads = ssd_d.shape[1]
    conv_dim = ssd_conv_w.shape[2]
    dt_col = 4 * width + 3 * width + width + conv_dim
    ret_col, sb_col = 0, 4 * width
    z_col, xbc_col = 7 * width, 8 * width
    assert dt_col % LANES == 0 and ssd_heads <= LANES

    n_in = w_in.shape[2]
    w_in_t = jnp.swapaxes(w_in, 1, 2).reshape(depth * n_in, d)
    w_ret, w_sb, w_ssd = (w.astype(BF16) for w in (w_branch_ret, w_branch_sb, w_branch_ssd))
    w_out_b = w_out.astype(BF16)
    w_down_b = ffn_w_down.astype(BF16)

    xf = x.reshape(s, d)
    cos, sin = rotary_tables(positions.reshape(s, 1), TABLE_ROWS)

    h = rms_norm(xf, norm_mix_pre[0].reshape(1, d), TABLE_ROWS)
    for l in range(depth):
        proj = matmul_nt(h, w_in_t, l * n_in, dt_col, BF16, STREAM_ROWS, STREAM_COLS)
        dt_raw = matmul_nt(h, w_in_t, l * n_in + dt_col, LANES, F32, STREAM_ROWS, LANES)
        gate_logits = matmul_nt(h, w_in_t, l * n_in + dt_col + ssd_heads, 3 * d, BF16, STREAM_ROWS, STREAM_COLS)

        y_ret = retention(proj, ret_col, cos, sin, ret_gn_w[l].reshape(1, width), ret_heads)
        y_sb = stickbreak(proj, sb_col, sb_heads)
        y_ssd = ssd(proj, z_col, xbc_col, dt_raw, ssd_conv_w[l], ssd_conv_b[l].reshape(1, conv_dim),
                    _pad_lanes(ssd_dt_bias[l]), _pad_lanes(ssd_a_log[l]),
                    jnp.repeat(ssd_d[l], SSD_HEAD_DIM).reshape(1, width),
                    ssd_norm_w[l].reshape(1, width), width, ssd_heads)

        merged = gated_merge(y_ret, y_sb, y_ssd, w_ret, w_sb, w_ssd, l, gate_logits,
                             b_gate[l].reshape(1, 3 * d), RESIDENT_ROWS, d)
        xf, h = matmul_norm_res(merged, w_out_b, l, xf, norm_mix_post[l].reshape(1, d),
                                norm_ffn_pre[l].reshape(1, d), RESIDENT_ROWS, d)

        act = swiglu(h, ffn_w_gate, ffn_w_up, l, STREAM_ROWS, STREAM_COLS)
        g_next = norm_mix_pre[l + 1].reshape(1, d) if l + 1 < depth else None
        xf, h = matmul_norm_res(act, w_down_b, l, xf, norm_ffn_post[l].reshape(1, d), g_next, RESIDENT_ROWS, d)

    return xf.reshape(b, s, d)
```

```python
import functools
import math

import jax
import jax.numpy as jnp
from jax import lax
from jax.experimental import pallas as pl
from jax.experimental.pallas import tpu as pltpu

F32 = jnp.float32
BF16 = jnp.bfloat16

NORM_EPS = 1e-6
ROPE_BASE = 10000.0
LOG2E = math.log2(math.e)
LANES = 128
SUBLANES = 8
HEAD_DIM = 128
RET_CHUNK = 64
RET_BLOCK = 256
SB_QUERIES = 1024
SB_KEYS = 256
SB_DEAD = -104.0
SSD_BLOCK = 128
SSD_HEAD_DIM = 64
SSD_GROUPS = 4
SSD_STATE = 128
SSD_CONV = 4
STREAM_ROWS = 2048
STREAM_COLS = 1024
SWIGLU_COLS = 512
RESIDENT_ROWS = 512
TABLE_ROWS = 1024
VMEM_LIMIT = 56 * 1024 * 1024


VMEM_LIMIT_RESIDENT = 60 * 1024 * 1024


def _params(sem, vmem_limit=VMEM_LIMIT):
    return pltpu.CompilerParams(dimension_semantics=sem, vmem_limit_bytes=vmem_limit)


def _sigmoid(x):
    return 1.0 / (1.0 + jnp.exp(-x))


def _silu(x):
    return x * _sigmoid(x)


def _log1p(e):
    u = 1.0 + e
    d = u - 1.0
    return jnp.where(d == 0.0, e, jnp.log(u) * (e / jnp.where(d == 0.0, 1.0, d)))


def _softplus(x):
    return jnp.maximum(x, 0.0) + _log1p(jnp.exp(-jnp.abs(x)))


def _split3(x):
    hi = x.astype(BF16)
    r1 = x - hi.astype(F32)
    mid = r1.astype(BF16)
    lo = (r1 - mid.astype(F32)).astype(BF16)
    return hi, mid, lo


def _dot(a, b):
    return jnp.dot(a, b, preferred_element_type=F32)


def _dot_nt(a, b):
    return lax.dot_general(a, b, (((1,), (1,)), ((), ())), preferred_element_type=F32)


def _rms_norm_kernel(x_ref, g_ref, h_ref):
    x = x_ref[...]
    ms = jnp.mean(x * x, axis=-1, keepdims=True)
    h_ref[...] = (x * lax.rsqrt(ms + NORM_EPS) * g_ref[...]).astype(h_ref.dtype)


def rms_norm(x, g, tm):
    m, d = x.shape
    return pl.pallas_call(
        _rms_norm_kernel,
        grid=(m // tm,),
        in_specs=[pl.BlockSpec((tm, d), lambda i: (i, 0)), pl.BlockSpec((1, d), lambda i: (0, 0))],
        out_specs=pl.BlockSpec((tm, d), lambda i: (i, 0)),
        out_shape=jax.ShapeDtypeStruct((m, d), BF16),
        compiler_params=_params(("parallel",)),
        name="rms_norm",
    )(x, g)


def _matmul_nt_kernel(h_ref, wt_ref, o_ref):
    o_ref[...] = _dot_nt(h_ref[...], wt_ref[...].astype(BF16)).astype(o_ref.dtype)


def matmul_nt(h, wt, row0, n, out_dtype, tm, tn):
    m, d = h.shape
    assert row0 % SUBLANES == 0 and tn % SUBLANES == 0
    return pl.pallas_call(
        _matmul_nt_kernel,
        grid=(m // tm, n // tn),
        in_specs=[pl.BlockSpec((tm, d), lambda i, j: (i, 0)),
                  pl.BlockSpec((pl.Element(tn), pl.Element(d)),
                               lambda i, j: (pl.multiple_of(row0 + j * tn, SUBLANES), 0))],
        out_specs=pl.BlockSpec((tm, tn), lambda i, j: (i, j)),
        out_shape=jax.ShapeDtypeStruct((m, n), out_dtype),
        compiler_params=_params(("parallel", "arbitrary")),
        name="matmul_nt",
    )(h, wt)


def _swiglu_kernel(h_ref, wg_ref, wu_ref, o_ref):
    wg = wg_ref[...].astype(BF16)
    wu = wu_ref[...].astype(BF16)
    half = h_ref.shape[0] // 2
    for r in range(2):
        rows = slice(r * half, (r + 1) * half)
        h = h_ref[rows, :]
        o_ref[rows, :] = (_silu(_dot(h, wg)) * _dot(h, wu)).astype(o_ref.dtype)


def swiglu(h, wg_stack, wu_stack, layer, tm, tn):
    m, d = h.shape
    n = wg_stack.shape[2]
    w_spec = pl.BlockSpec((None, d, tn), lambda i, j: (layer, 0, j))
    return pl.pallas_call(
        _swiglu_kernel,
        grid=(m // tm, n // tn),
        in_specs=[pl.BlockSpec((tm, d), lambda i, j: (i, 0)), w_spec, w_spec],
        out_specs=pl.BlockSpec((tm, tn), lambda i, j: (i, j)),
        out_shape=jax.ShapeDtypeStruct((m, n), BF16),
        compiler_params=_params(("parallel", "arbitrary")),
        name="swiglu",
    )(h, wg_stack, wu_stack)


def _matmul_norm_res_kernel(a_ref, w_ref, x_ref, g_ref, *rest):
    emit_next = len(rest) == 4
    gn_ref, o_ref, h_ref, f_ref = rest if emit_next else (None, rest[0], None, rest[1])
    j = pl.program_id(1)
    nj, _, tn = f_ref.shape
    f_ref[j] = _dot(a_ref[...], w_ref[...])

    @pl.when(j == nj - 1)
    def _():
        ss = jnp.zeros((f_ref.shape[1], 1), F32)
        for c in range(nj):
            f = f_ref[c]
            ss = ss + jnp.sum(f * f, axis=-1, keepdims=True)
        inv = lax.rsqrt(ss / (nj * tn) + NORM_EPS)
        ss = jnp.zeros((f_ref.shape[1], 1), F32)
        for c in range(nj):
            cols = slice(c * tn, (c + 1) * tn)
            y = x_ref[:, cols] + f_ref[c] * inv * g_ref[:, cols]
            o_ref[:, cols] = y
            ss = ss + jnp.sum(y * y, axis=-1, keepdims=True)
        if emit_next:
            inv = lax.rsqrt(ss / (nj * tn) + NORM_EPS)
            for c in range(nj):
                cols = slice(c * tn, (c + 1) * tn)
                h_ref[:, cols] = (o_ref[:, cols] * inv * gn_ref[:, cols]).astype(h_ref.dtype)


def matmul_norm_res(a, w_stack, layer, x, g, g_next, tm, tn):
    m, kdim = a.shape
    d = w_stack.shape[2]
    row_spec = pl.BlockSpec((tm, d), lambda i, j: (i, 0))
    vec_spec = pl.BlockSpec((1, d), lambda i, j: (0, 0))
    emit_next = g_next is not None
    outs = pl.pallas_call(
        _matmul_norm_res_kernel,
        grid=(m // tm, d // tn),
        in_specs=[pl.BlockSpec((tm, kdim), lambda i, j: (i, 0)),
                  pl.BlockSpec((None, kdim, tn), lambda i, j: (layer, 0, j),
                               **({"pipeline_mode": pl.Buffered(1)} if tn == d else {})),
                  row_spec, vec_spec] + ([vec_spec] if emit_next else []),
        out_specs=[row_spec] + ([row_spec] if emit_next else []),
        out_shape=[jax.ShapeDtypeStruct((m, d), F32)] + ([jax.ShapeDtypeStruct((m, d), BF16)] if emit_next else []),
        scratch_shapes=[pltpu.VMEM((d // tn, tm, tn), F32)],
        compiler_params=_params(("parallel", "arbitrary"), VMEM_LIMIT_RESIDENT if tn == d else VMEM_LIMIT),
        name="matmul_norm_res",
    )(a, w_stack, x, g, *([g_next] if emit_next else []))
    return (outs[0], outs[1]) if emit_next else (outs[0], None)


def _gated_merge_kernel(yr_ref, ys_ref, yd_ref, wr_ref, ws_ref, wd_ref,
                        gr_ref, gs_ref, gd_ref, br_ref, bs_ref, bd_ref, o_ref):
    def branch(y_ref, w_ref, gl_ref, b_ref):
        gate = _sigmoid(gl_ref[...].astype(F32) + b_ref[...])
        return gate * _dot(y_ref[...], w_ref[...])

    o_ref[...] = (branch(yr_ref, wr_ref, gr_ref, br_ref)
                  + branch(ys_ref, ws_ref, gs_ref, bs_ref)
                  + branch(yd_ref, wd_ref, gd_ref, bd_ref)).astype(o_ref.dtype)


def gated_merge(y_ret, y_sb, y_ssd, w_ret, w_sb, w_ssd, layer, gate_logits, b_gate, tm, tn):
    m, width = y_ret.shape
    d = w_ret.shape[2]
    nb = d // tn
    y_spec = pl.BlockSpec((tm, width), lambda i, j: (i, 0))
    w_spec = pl.BlockSpec((None, width, tn), lambda i, j: (layer, 0, j),
                          **({"pipeline_mode": pl.Buffered(1)} if tn == d else {}))

    def gate_spec(b):
        return pl.BlockSpec((tm, tn), lambda i, j: (i, b * nb + j))

    def bias_spec(b):
        return pl.BlockSpec((1, tn), lambda i, j: (0, b * nb + j))

    return pl.pallas_call(
        _gated_merge_kernel,
        grid=(m // tm, nb),
        in_specs=[y_spec, y_spec, y_spec, w_spec, w_spec, w_spec,
                  gate_spec(0), gate_spec(1), gate_spec(2),
                  bias_spec(0), bias_spec(1), bias_spec(2)],
        out_specs=pl.BlockSpec((tm, tn), lambda i, j: (i, j)),
        out_shape=jax.ShapeDtypeStruct((m, d), BF16),
        compiler_params=_params(("parallel", "arbitrary")),
        name="gated_merge",
    )(y_ret, y_sb, y_ssd, w_ret, w_sb, w_ssd, gate_logits, gate_logits, gate_logits, b_gate, b_gate, b_gate)


def _rotary_kernel(pos_ref, freq_ref, cos_ref, sin_ref):
    ang = pos_ref[...].astype(F32) * freq_ref[...]
    lane = lax.broadcasted_iota(jnp.int32, ang.shape, 1)
    cos_ref[...] = jnp.cos(ang)
    sin_ref[...] = jnp.where(lane < HEAD_DIM // 2, -jnp.sin(ang), jnp.sin(ang))


def rotary_tables(positions_col, ts):
    s = positions_col.shape[0]
    half = HEAD_DIM // 2
    inv_freq = ROPE_BASE ** (-2.0 * jnp.arange(half, dtype=F32) / HEAD_DIM)
    freq = jnp.concatenate([inv_freq, inv_freq]).reshape(1, HEAD_DIM)
    return pl.pallas_call(
        _rotary_kernel,
        grid=(s // ts,),
        in_specs=[pl.BlockSpec((ts, 1), lambda i: (i, 0)),
                  pl.BlockSpec((1, HEAD_DIM), lambda i: (0, 0))],
        out_specs=[pl.BlockSpec((ts, HEAD_DIM), lambda i: (i, 0)),
                   pl.BlockSpec((ts, HEAD_DIM), lambda i: (i, 0))],
        out_shape=[jax.ShapeDtypeStruct((s, HEAD_DIM), F32),
                   jax.ShapeDtypeStruct((s, HEAD_DIM), F32)],
        compiler_params=_params(("parallel",)),
        name="rotary_tables",
    )(positions_col, freq)


def _retention_kernel(q_ref, k_ref, v_ref, g_ref, cos_ref, sin_ref, lg_ref, gn_ref, o_ref,
                      state_ref, decay_ref, qd_ref, kd_ref, *, heads):
    rb = RET_BLOCK

    @pl.when(pl.program_id(0) == 0)
    def _():
        state_ref[...] = jnp.zeros_like(state_ref)
        t = lax.broadcasted_iota(jnp.int32, (rb, rb), 0)
        s = lax.broadcasted_iota(jnp.int32, (rb, rb), 1)
        dist = jnp.abs(t - s).astype(F32)
        visible = (s // RET_CHUNK) <= (t // RET_CHUNK)
        row = lax.broadcasted_iota(jnp.int32, (rb, HEAD_DIM), 0).astype(F32)
        for h in range(heads):
            lg = lg_ref[h]
            decay_ref[h] = jnp.where(visible, jnp.exp(lg[:, :1] * dist), 0.0)
            qd_ref[h] = jnp.exp(lg * (row + 1.0))
            kd_ref[h] = jnp.exp(lg * (rb - 1.0 - row))

    cos = cos_ref[...]
    sin = sin_ref[...]

    def rotate(t):
        return t * cos + pltpu.roll(t, HEAD_DIM // 2, 1) * sin

    for h in range(heads):
        lanes = slice(h * HEAD_DIM, (h + 1) * HEAD_DIM)
        q = rotate(q_ref[:, lanes].astype(F32))
        k = rotate(k_ref[:, lanes].astype(F32)) * (HEAD_DIM ** -0.5)
        v = v_ref[:, lanes]
        scores = _dot_nt(q.astype(BF16), k.astype(BF16)) * decay_ref[h]
        o = _dot(scores.astype(BF16), v)
        state = state_ref[h]
        o = o + _dot((q * qd_ref[h]).astype(BF16), state.astype(BF16))
        kdt = jnp.transpose(k * kd_ref[h]).astype(BF16)
        block_decay = jnp.exp(lg_ref[h] * float(rb))
        state_ref[h] = state * block_decay + _dot(kdt, v)

        mu = jnp.mean(o, axis=-1, keepdims=True)
        oc = o - mu
        var = jnp.mean(oc * oc, axis=-1, keepdims=True)
        o = oc * lax.rsqrt(var + NORM_EPS) * gn_ref[:, lanes]
        o_ref[:, lanes] = (o * _silu(g_ref[:, lanes].astype(F32))).astype(o_ref.dtype)


def retention(proj, col0, cos, sin, gn_w, heads):
    s = proj.shape[0]
    rb = RET_BLOCK
    width = heads * HEAD_DIM
    c0 = col0 // width
    log_gamma = jnp.log1p(-jnp.exp2(-5.0 - jnp.arange(heads, dtype=F32)))
    lg = jnp.broadcast_to(log_gamma[:, None, None], (heads, 1, HEAD_DIM))

    def col_spec(part):
        return pl.BlockSpec((rb, width), lambda r: (r, c0 + part))

    tab_spec = pl.BlockSpec((rb, HEAD_DIM), lambda r: (r, 0))
    return pl.pallas_call(
        functools.partial(_retention_kernel, heads=heads),
        grid=(s // rb,),
        in_specs=[col_spec(0), col_spec(1), col_spec(2), col_spec(3), tab_spec, tab_spec,
                  pl.BlockSpec((heads, 1, HEAD_DIM), lambda r: (0, 0, 0)),
                  pl.BlockSpec((1, width), lambda r: (0, 0))],
        out_specs=pl.BlockSpec((rb, width), lambda r: (r, 0)),
        out_shape=jax.ShapeDtypeStruct((s, width), BF16),
        scratch_shapes=[pltpu.VMEM((heads, HEAD_DIM, HEAD_DIM), F32),
                        pltpu.VMEM((heads, rb, rb), F32),
                        pltpu.VMEM((heads, rb, HEAD_DIM), F32),
                        pltpu.VMEM((heads, rb, HEAD_DIM), F32)],
        compiler_params=_params(("arbitrary",)),
        name="retention",
    )(proj, proj, proj, proj, cos, sin, lg, gn_w)


def _stickbreak_kernel(q_ref, k_ref, v_ref, o_ref, acc_ref, carry_ref, *, tq):
    tk = SB_KEYS
    hd = HEAD_DIM
    i = pl.program_id(1)
    scale = hd ** -0.5
    krow = lax.broadcasted_iota(jnp.int32, (tk, tk), 0)
    kcol = lax.broadcasted_iota(jnp.int32, (tk, tk), 1)
    neg_suffix = jnp.where(krow >= kcol, -1.0, 0.0).astype(BF16)
    neg_suffix2 = jnp.concatenate([neg_suffix, neg_suffix], axis=0)
    zeros = jnp.zeros((tk, hd), BF16)

    def block_diag(x):
        return jnp.concatenate([jnp.concatenate([x[:, :hd], zeros], axis=1),
                                jnp.concatenate([zeros, x[:, hd:]], axis=1)], axis=0)

    acc_ref[...] = jnp.zeros_like(acc_ref)
    carry_ref[...] = jnp.zeros_like(carry_ref)

    def key_rows(key_block):
        return pl.ds(pl.multiple_of(key_block * tk, tk), tk)

    def causal_mask(rows):
        key = lax.broadcasted_iota(jnp.int32, (rows, 2 * tk), 1) & (tk - 1)
        return key < lax.broadcasted_iota(jnp.int32, (rows, 2 * tk), 0)

    def scores(key_block, row0, rows):
        return _dot_nt(q_ref[row0:row0 + rows, :], block_diag(k_ref[key_rows(key_block), :])) * scale

    def suffix_sums(z, rows, masked):
        sp = jnp.maximum(z, 0.0) + jnp.log(1.0 + jnp.exp2(jnp.abs(z) * -LOG2E))
        if masked:
            sp = jnp.where(causal_mask(rows), sp, 0.0)
        hi = sp.astype(BF16)
        lo = (sp - hi.astype(F32)).astype(BF16)
        suffix = [_dot(jnp.concatenate([hi[:, h * tk:(h + 1) * tk], lo[:, h * tk:(h + 1) * tk]], axis=1),
                       neg_suffix2) for h in range(2)]
        totals = jnp.concatenate([jnp.broadcast_to(sfx[:, :1], (rows, hd)) for sfx in suffix], axis=1)
        return z + jnp.concatenate(suffix, axis=1), totals

    def attend(key_block, row0, rows, masked, arg, totals, valid=None):
        carry = carry_ref[row0:row0 + rows, :]
        c0, c1 = carry[:, :hd], carry[:, hd:]
        reps = tk // LANES
        w = jnp.exp(arg + jnp.concatenate([c0] * reps + [c1] * reps, axis=1))
        if masked:
            w = jnp.where(causal_mask(rows), w, 0.0)
        if valid is not None:
            w = jnp.where(valid, w, 0.0)
            totals = jnp.where(valid, totals, 0.0)
        acc_ref[row0:row0 + rows, :] += _dot(w.astype(BF16), block_diag(v_ref[key_rows(key_block), :]))
        carry_ref[row0:row0 + rows, :] = carry + totals

    def tile(key_block, row0, rows):
        attend(key_block, row0, rows, False, *suffix_sums(scores(key_block, row0, rows), rows, False))

    def alive(row0, rows):
        return jnp.max(carry_ref[row0:row0 + rows, :]) > SB_DEAD

    bands = tq // tk
    n_full = i * bands
    near = [(i * bands + b, b * tk, min(2 * tk, tq - b * tk), True) for b in range(bands - 1, -1, -1)]
    near.append((jnp.maximum(n_full - 1, 0), 0, tk, False))
    zs = [scores(kb, row0, rows) for kb, row0, rows, _ in near]
    sums = [suffix_sums(z, rows, masked) for z, (_, _, rows, masked) in zip(zs, near)]
    for (kb, row0, rows, masked), (arg, totals) in zip(near, sums):
        attend(kb, row0, rows, masked, arg, totals, None if masked else n_full > 0)
    for b in range(bands - 1, -1, -1):
        far0 = (b + 2) * tk
        if far0 < tq:
            @pl.when(alive(far0, tq - far0))
            def _():
                tile(i * bands + b, far0, tq - far0)

    @pl.when(jnp.logical_and(n_full > 0, alive(tk, tq - tk)))
    def _():
        tile(n_full - 1, tk, tq - tk)

    def body(n):
        tile(n_full - 1 - n, 0, tq)
        return n + 1

    lax.while_loop(lambda n: jnp.logical_and(n < n_full, alive(0, tq)), body, 1)
    o_ref[...] = acc_ref[...].astype(o_ref.dtype)


def stickbreak(proj, col0, heads):
    s = proj.shape[0]
    tq = min(SB_QUERIES, s)
    wide = 2 * HEAD_DIM
    c0 = col0 // wide
    pairs = heads // 2
    return pl.pallas_call(
        functools.partial(_stickbreak_kernel, tq=tq),
        grid=(pairs, s // tq),
        in_specs=[pl.BlockSpec((tq, wide), lambda g, i: (i, c0 + g)),
                  pl.BlockSpec((s, wide), lambda g, i: (0, c0 + pairs + g)),
                  pl.BlockSpec((s, wide), lambda g, i: (0, c0 + 2 * pairs + g))],
        out_specs=pl.BlockSpec((tq, wide), lambda g, i: (i, g)),
        out_shape=jax.ShapeDtypeStruct((s, heads * HEAD_DIM), BF16),
        scratch_shapes=[pltpu.VMEM((tq, wide), F32), pltpu.VMEM((tq, wide), F32)],
        compiler_params=_params(("parallel", "arbitrary")),
        name="stickbreak",
    )(proj, proj, proj)


def _ssd_kernel(xbc_ref, z_ref, dt_ref, cw_ref, cb_ref, dtb_ref, alog_ref, dskip_ref, nw_ref, o_ref,
                win_ref, state_ref, y_ref, *, width, heads):
    lb = SSD_BLOCK
    heads_per_group = heads // SSD_GROUPS

    @pl.when(pl.program_id(0) == 0)
    def _():
        state_ref[...] = jnp.zeros_like(state_ref)
        win_ref[...] = jnp.zeros_like(win_ref)

    u = xbc_ref[...]
    both = jnp.concatenate([u, win_ref[...]], axis=0)
    t = lax.broadcasted_iota(jnp.int32, (lb, 2 * lb), 0)
    src = lax.broadcasted_iota(jnp.int32, (lb, 2 * lb), 1)
    conv = cb_ref[...] + cw_ref[SSD_CONV - 1:SSD_CONV, :] * u.astype(F32)
    for back in range(1, SSD_CONV):
        wanted = jnp.where(t >= back, t - back, 2 * lb + t - back)
        shifted = _dot(jnp.where(src == wanted, 1.0, 0.0).astype(BF16), both)
        conv = conv + cw_ref[SSD_CONV - 1 - back:SSD_CONV - back, :] * shifted
    win_ref[...] = u
    act = _silu(conv)
    xs = act[:, :width]
    gn = SSD_GROUPS * SSD_STATE
    bm = act[:, width:width + gn]
    cm = act[:, width + gn:width + 2 * gn]

    dt = _softplus(dt_ref[...] + dtb_ref[...])
    a = dt * (-jnp.exp(alog_ref[...]))
    row = lax.broadcasted_iota(jnp.int32, (lb, lb), 0)
    col = lax.broadcasted_iota(jnp.int32, (lb, lb), 1)
    causal = col <= row
    tri = jnp.where(causal, 1.0, 0.0).astype(BF16)
    a_hi, a_mid, a_lo = _split3(a)
    acum = _dot(tri, a_hi) + _dot(tri, a_mid) + _dot(tri, a_lo)
    acum_t = jnp.transpose(acum)
    lane = lax.broadcasted_iota(jnp.int32, (lb, LANES), 1)
    low_half = lane < SSD_HEAD_DIM

    for g in range(SSD_GROUPS):
        bg = bm[:, g * SSD_STATE:(g + 1) * SSD_STATE]
        cg = cm[:, g * SSD_STATE:(g + 1) * SSD_STATE].astype(BF16)
        cb = jnp.where(causal, _dot_nt(cg, bg.astype(BF16)), 0.0)
        bg_t = jnp.transpose(bg).astype(BF16)
        for pp in range(heads_per_group // 2):
            p = g * (heads_per_group // 2) + pp
            e0, e1 = 2 * p, 2 * p + 1
            m_parts, acol_parts, dt_parts = [], [], []
            for e in (e0, e1):
                acol = jnp.broadcast_to(acum[:, e:e + 1], (lb, lb))
                seg = acol - acum_t[e:e + 1, :]
                decay = jnp.exp(jnp.where(causal, seg, 0.0))
                m_parts.append((cb * decay).astype(BF16))
                acol_parts.append(acol)
                dt_parts.append(jnp.broadcast_to(dt[:, e:e + 1], (lb, LANES)))
            acol_pair = jnp.where(low_half, acol_parts[0], acol_parts[1])
            dt_pair = jnp.where(low_half, dt_parts[0], dt_parts[1])
            x_pair = xs[:, p * LANES:(p + 1) * LANES]
            xdt = x_pair * dt_pair
            xdt_lo = jnp.where(low_half, xdt, 0.0).astype(BF16)
            xdt_hi = jnp.where(low_half, 0.0, xdt).astype(BF16)
            y = _dot(jnp.concatenate(m_parts, axis=1), jnp.concatenate([xdt_lo, xdt_hi], axis=0))
            state = state_ref[p]
            y = y + _dot(cg, state.astype(BF16)) * jnp.exp(acol_pair)
            last = acol_pair[lb - 1:lb, :]
            to_end = jnp.exp(last - acol_pair)
            state_ref[p] = state * jnp.exp(last) + _dot(bg_t, (xdt * to_end).astype(BF16))
            y = y + x_pair * dskip_ref[:, p * LANES:(p + 1) * LANES]
            y_ref[:, p * LANES:(p + 1) * LANES] = y

    yz = y_ref[...] * _silu(z_ref[...].astype(F32))
    ms = jnp.mean(yz * yz, axis=-1, keepdims=True)
    o_ref[...] = (yz * lax.rsqrt(ms + NORM_EPS) * nw_ref[...]).astype(o_ref.dtype)


def ssd(proj, z_col0, xbc_col0, dt_raw, conv_w, conv_b, dt_bias, a_log, d_skip, norm_w, width, heads):
    s = proj.shape[0]
    lb = SSD_BLOCK
    conv_dim = width + 2 * SSD_GROUPS * SSD_STATE
    row_vec = lambda n: pl.BlockSpec((1, n), lambda r: (0, 0))
    kernel = functools.partial(_ssd_kernel, width=width, heads=heads)
    return pl.pallas_call(
        kernel,
        grid=(s // lb,),
        in_specs=[pl.BlockSpec((lb, conv_dim), lambda r: (r, xbc_col0 // conv_dim)),
                  pl.BlockSpec((lb, width), lambda r: (r, z_col0 // width)),
                  pl.BlockSpec((lb, LANES), lambda r: (r, 0)),
                  pl.BlockSpec((SSD_CONV, conv_dim), lambda r: (0, 0)),
                  row_vec(conv_dim), row_vec(LANES), row_vec(LANES), row_vec(width), row_vec(width)],
        out_specs=pl.BlockSpec((lb, width), lambda r: (r, 0)),
        out_shape=jax.ShapeDtypeStruct((s, width), BF16),
        scratch_shapes=[pltpu.VMEM((lb, conv_dim), BF16),
                        pltpu.VMEM((heads // 2, SSD_STATE, LANES), F32),
                        pltpu.VMEM((lb, width), F32)],
        compiler_params=_params(("arbitrary",)),
        name="ssd",
    )(proj, proj, dt_raw, conv_w, conv_b, dt_bias, a_log, d_skip, norm_w)


def _pad_lanes(v):
    return jnp.pad(v.astype(F32), (0, LANES - v.shape[0])).reshape(1, LANES)


def kernel(x, positions, norm_mix_pre, norm_mix_post, norm_ffn_pre, norm_ffn_post, w_in, b_gate, ret_gn_w, ssd_conv_w, ssd_conv_b, ssd_dt_bias, ssd_a_log, ssd_d, ssd_norm_w, w_branch_ret, w_branch_sb, w_branch_ssd, w_out, ffn_w_gate, ffn_w_up, ffn_w_down):
    b, s, d = x.shape
    assert b == 1
    depth = w_in.shape[0]
    width = w_branch_ret.shape[1]
    ret_heads = width // HEAD_DIM
    sb_heads = width // HEAD_DIM
    ssd_heads = ssd_d.shape[1]
    conv_dim = ssd_conv_w.shape[2]
    dt_col = 4 * width + 3 * width + width + conv_dim
    ret_col, sb_col = 0, 4 * width
    z_col, xbc_col = 7 * width, 8 * width
    assert dt_col % LANES == 0 and ssd_heads <= LANES

    n_in = w_in.shape[2]
    w_in_t = jnp.swapaxes(w_in, 1, 2).reshape(depth * n_in, d)
    w_ret, w_sb, w_ssd = (w.astype(BF16) for w in (w_branch_ret, w_branch_sb, w_branch_ssd))
    w_out_b = w_out.astype(BF16)
    w_down_b = ffn_w_down.astype(BF16)

    xf = x.reshape(s, d)
    cos, sin = rotary_tables(positions.reshape(s, 1), TABLE_ROWS)

    h = rms_norm(xf, norm_mix_pre[0].reshape(1, d), TABLE_ROWS)
    for l in range(depth):
        proj = matmul_nt(h, w_in_t, l * n_in, dt_col, BF16, STREAM_ROWS, STREAM_COLS)
        dt_raw = matmul_nt(h, w_in_t, l * n_in + dt_col, LANES, F32, STREAM_ROWS, LANES)
        gate_logits = matmul_nt(h, w_in_t, l * n_in + dt_col + ssd_heads, 3 * d, BF16, STREAM_ROWS, STREAM_COLS)

        y_ret = retention(proj, ret_col, cos, sin, ret_gn_w[l].reshape(1, width), ret_heads)
        y_sb = stickbreak(proj, sb_col, sb_heads)
        y_ssd = ssd(proj, z_col, xbc_col, dt_raw, ssd_conv_w[l], ssd_conv_b[l].reshape(1, conv_dim),
                    _pad_lanes(ssd_dt_bias[l]), _pad_lanes(ssd_a_log[l]),
                    jnp.repeat(ssd_d[l], SSD_HEAD_DIM).reshape(1, width),
                    ssd_norm_w[l].reshape(1, width), width, ssd_heads)

        merged = gated_merge(y_ret, y_sb, y_ssd, w_ret, w_sb, w_ssd, l, gate_logits,
                             b_gate[l].reshape(1, 3 * d), RESIDENT_ROWS, d)
        xf, h = matmul_norm_res(merged, w_out_b, l, xf, norm_mix_post[l].reshape(1, d),
                                norm_ffn_pre[l].reshape(1, d), RESIDENT_ROWS, d)

        act = swiglu(h, ffn_w_gate, ffn_w_up, l, STREAM_ROWS, SWIGLU_COLS)
        g_next = norm_mix_pre[l + 1].reshape(1, d) if l + 1 < depth else None
        xf, h = matmul_norm_res(act, w_down_b, l, xf, norm_ffn_post[l].reshape(1, d), g_next, RESIDENT_ROWS, d)

    return xf.reshape(b, s, d)
```
